```python
import math
import jax, jax.numpy as jnp
from jax import lax
import numpy as np

D_MODEL = 1024
BATCH = 4
SEQ = 8192
DEPTH = 4

GRID_W = 64
CTX_LEN = 256
N_MIXERS = 3
A_HEADS = 8
A_KV_HEADS = 2
A_HEAD_DIM = 128
ROPE_THETA = 10000.0
Q_BLOCK = 128
B_CHUNK = 128
B_GROUPS = 8
B_WIDTH = D_MODEL
C_HEADS = 8
C_HEAD_DIM = D_MODEL // C_HEADS
C_CONV = 3
C_CHUNK = 64
N_EXPERTS = 16
N_GROUPS = 4
TOPK_GROUPS = 1
TOP_K = 2
D_EXPERT = 512
MOE_BLOCK = 256

EPS = 1e-6
DN_ALPHA = (2 * DEPTH) ** 0.25
DN_BETA = (8 * DEPTH) ** -0.25

kernel_name = "hybrid_interleaved_flow_backbone"


def layer_norm(x, g, b):
    xf = x.astype(jnp.float32)
    mu = jnp.mean(xf, -1, keepdims=True)
    var = jnp.mean(jnp.square(xf - mu), -1, keepdims=True)
    return ((xf - mu) * lax.rsqrt(var + EPS) * g + b).astype(x.dtype)


def rms_norm(x, g):
    xf = x.astype(jnp.float32)
    return (xf * lax.rsqrt(jnp.mean(xf * xf, -1, keepdims=True) + EPS) * g).astype(x.dtype)


def l2norm(x):
    return x * lax.rsqrt(jnp.sum(x * x, -1, keepdims=True) + EPS)


def modulate(x, shift, scale):
    return x * (1 + scale[:, None, :]) + shift[:, None, :]


def axial_rope(n):
    rows = n // GRID_W
    half = A_HEAD_DIM // 2
    inv = 1.0 / (ROPE_THETA ** (jnp.arange(0, half, 2, dtype=jnp.float32) / half))
    r = jnp.broadcast_to(jnp.arange(rows, dtype=jnp.float32)[:, None], (rows, GRID_W)).reshape(-1)
    col = jnp.broadcast_to(jnp.arange(GRID_W, dtype=jnp.float32)[None, :], (rows, GRID_W)).reshape(-1)
    ang = jnp.concatenate([r[:, None] * inv, col[:, None] * inv], -1)
    return jnp.cos(ang), jnp.sin(ang)


def apply_rope(x, cos, sin):
    xf = x.astype(jnp.float32).reshape(*x.shape[:-1], -1, 2)
    x1, x2 = xf[..., 0], xf[..., 1]
    c = cos[None, :, None, :]
    s = sin[None, :, None, :]
    return jnp.stack([x1 * c - x2 * s, x1 * s + x2 * c], -1).reshape(x.shape).astype(x.dtype)


def attention_mixer(u_lat, u_ctx, w_qkv, q_g, k_g, w_o, cos, sin, need_ctx):
    b, n, _ = u_lat.shape
    grp = A_HEADS // A_KV_HEADS
    scale = A_HEAD_DIM ** -0.5

    def proj(h):
        qkv = h @ w_qkv
        q, k, v = jnp.split(qkv, [A_HEADS * A_HEAD_DIM, (A_HEADS + A_KV_HEADS) * A_HEAD_DIM], -1)
        q = rms_norm(q.reshape(*h.shape[:2], A_HEADS, A_HEAD_DIM), q_g)
        k = rms_norm(k.reshape(*h.shape[:2], A_KV_HEADS, A_HEAD_DIM), k_g)
        return q, k, v.reshape(*h.shape[:2], A_KV_HEADS, A_HEAD_DIM)

    def attend(q, k, v):
        nq = q.shape[1]
        qg = q.reshape(b, nq, A_KV_HEADS, grp, A_HEAD_DIM)
        s = jnp.einsum("bqkgd,bskd->bkgqs", qg, k, preferred_element_type=jnp.float32) * scale
        p = jax.nn.softmax(s, axis=-1).astype(v.dtype)
        o = jnp.einsum("bkgqs,bskd->bqkgd", p, v)
        return o.reshape(b, nq, A_HEADS * A_HEAD_DIM)

    q_l, k_l, v_l = proj(u_lat)
    q_c, k_c, v_c = proj(u_ctx)
    q_l = apply_rope(q_l, cos, sin)
    k_l = apply_rope(k_l, cos, sin)
    k_all = jnp.concatenate([k_c, k_l], 1)
    v_all = jnp.concatenate([v_c, v_l], 1)
    nb = n // Q_BLOCK
    q_blocks = jnp.moveaxis(q_l.reshape(b, nb, Q_BLOCK, A_HEADS, A_HEAD_DIM), 1, 0)
    o_l = lax.map(lambda qb: attend(qb, k_all, v_all), q_blocks)
    y_lat = jnp.moveaxis(o_l, 0, 1).reshape(b, n, A_HEADS * A_HEAD_DIM) @ w_o
    y_ctx = attend(q_c, k_c, v_c) @ w_o if need_ctx else None
    return y_lat, y_ctx


def chunk_mlp(h, w_in, v_g, v_b, w_s, b_s, w_out):
    b, n, _ = h.shape
    z = jax.nn.gelu(h @ w_in)
    u, v = jnp.split(z, 2, -1)
    v = layer_norm(v, v_g, v_b)
    vg = v.reshape(b, n // B_CHUNK, B_CHUNK, B_GROUPS, B_WIDTH // B_GROUPS)
    mixed = jnp.einsum("gts,bcsgd->bctgd", w_s, vg) + b_s.T[None, None, :, :, None]
    return (u * mixed.reshape(b, n, B_WIDTH)) @ w_out


def short_conv(x, w):
    pad = C_CONV // 2
    return lax.conv_general_dilated(x, w[:, None, :], window_strides=(1,), padding=[(pad, pad)],
                                    dimension_numbers=("NWC", "WIO", "NWC"), feature_group_count=x.shape[-1])


def gated_delta_chunked(q, k, v, g, beta, s0):
    b, n, nh, dk = k.shape
    dv = v.shape[-1]
    nc = n // C_CHUNK

    def to_chunks(t):
        return jnp.moveaxis(t.reshape(b, nc, C_CHUNK, nh, *t.shape[3:]), (1, 2), (0, 3))

    qc, kc, vc = to_chunks(q), to_chunks(k), to_chunks(v)
    gc = jnp.cumsum(to_chunks(g), -1)
    bc = to_chunks(beta)
    idx = jnp.arange(C_CHUNK)
    tril = idx[:, None] >= idx[None, :]
    strict = idx[:, None] > idx[None, :]
    decay = jnp.exp(jnp.where(tril, gc[..., :, None] - gc[..., None, :], -jnp.inf))
    kb = kc * bc[..., None]
    a_mat = jnp.where(strict, jnp.einsum("...id,...jd->...ij", kb, kc) * decay, 0.0)
    eye = jnp.eye(C_CHUNK, dtype=jnp.float32)
    t_inv = lax.linalg.triangular_solve(a_mat + eye, jnp.broadcast_to(eye, a_mat.shape),
                                        left_side=True, lower=True)
    u = t_inv @ (vc * bc[..., None])
    w = t_inv @ (kb * jnp.exp(gc)[..., None])
    attn = jnp.where(tril, jnp.einsum("...id,...jd->...ij", qc, kc) * decay, 0.0)
    q_dec = qc * jnp.exp(gc)[..., None]
    k_dec = kc * jnp.exp(gc[..., -1:] - gc)[..., None]
    g_last = jnp.exp(gc[..., -1])

    def step(s, xs):
        u_i, w_i, attn_i, qd_i, kd_i, gl_i = xs
        v_new = u_i - w_i @ s
        o = qd_i @ s + attn_i @ v_new
        s = s * gl_i[..., None, None] + jnp.swapaxes(kd_i, -1, -2) @ v_new
        return s, o

    s_fin, o = lax.scan(step, s0, (u, w, attn, q_dec, k_dec, g_last))
    o = jnp.moveaxis(o, (0, 3), (1, 2)).reshape(b, n, nh, dv)
    return o, s_fin


def deltanet_mixer(u_lat, u_ctx, w_in, conv_w, a_log, dt_bias, norm_g, w_out, need_ctx):
    cw = C_HEADS * C_HEAD_DIM
    f32 = jnp.float32

    def prep(h):
        b, n, _ = h.shape
        proj = h @ w_in
        qkv = jax.nn.silu(short_conv(proj[..., :3 * cw], conv_w)).astype(f32)
        q, k, v = [t.reshape(b, n, C_HEADS, C_HEAD_DIM) for t in jnp.split(qkv, 3, -1)]
        q = l2norm(q) * (C_HEAD_DIM ** -0.5)
        k = l2norm(k)
        z = proj[..., 3 * cw:4 * cw].astype(f32).reshape(b, n, C_HEADS, C_HEAD_DIM)
        ab = proj[..., 4 * cw:].astype(f32).reshape(b, n, 2, 2, C_HEADS)
        g = -jnp.exp(a_log.astype(f32)) * jax.nn.softplus(ab[:, :, 0] + dt_bias.astype(f32))
        beta = jax.nn.sigmoid(ab[:, :, 1])
        return q, k, v, g, beta, z

    def bidir(q, k, v, g, beta, s_f, s_b):
        flip = lambda t: jnp.flip(t, 1)
        o_f, s_f = gated_delta_chunked(q, k, v, g[:, :, 0], beta[:, :, 0], s_f)
        o_b, s_b = gated_delta_chunked(flip(q), flip(k), flip(v), flip(g[:, :, 1]), flip(beta[:, :, 1]), s_b)
        return o_f + flip(o_b), s_f, s_b

    def out(o, z, dtype):
        y = rms_norm(o, norm_g) * jax.nn.silu(z)
        return y.reshape(*y.shape[:2], cw).astype(dtype) @ w_out

    qc, kc, vc, gc, bc, zc = prep(u_ctx)
    s0 = jnp.zeros((u_ctx.shape[0], C_HEADS, C_HEAD_DIM, C_HEAD_DIM), f32)
    o_ctx, s_f, s_b = bidir(qc, kc, vc, gc, bc, s0, s0)
    ql, kl, vl, gl, bl, zl = prep(u_lat)
    o_lat, _, _ = bidir(ql, kl, vl, gl, bl, s_f, s_b)
    y_lat = out(o_lat, zl, u_lat.dtype)
    y_ctx = out(o_ctx, zc, u_ctx.dtype) if need_ctx else None
    return y_lat, y_ctx


def moe_ffn(t, w_router, router_bias, w_gate, w_up, w_down):
    n_tok, d = t.shape
    per = N_EXPERTS // N_GROUPS
    scores = jax.nn.sigmoid((t @ w_router).astype(jnp.float32))
    sel = scores + router_bias.astype(jnp.float32)
    grp_score = lax.top_k(sel.reshape(n_tok, N_GROUPS, per), 2)[0].sum(-1)
    _, gidx = lax.top_k(grp_score, TOPK_GROUPS)
    gmask = jnp.any(gidx[..., None] == jnp.arange(N_GROUPS), axis=1)
    emask = jnp.repeat(gmask, per, axis=1)
    _, eidx = lax.top_k(jnp.where(emask, sel, -jnp.inf), TOP_K)
    wts = jnp.take_along_axis(scores, eidx, 1)
    wts = wts / jnp.sum(wts, -1, keepdims=True)
    n_asg = n_tok * TOP_K
    e_flat = eidx.reshape(-1)
    tok_flat = jnp.repeat(jnp.arange(n_tok, dtype=jnp.int32), TOP_K)
    order = jnp.argsort(e_flat)
    e_s, tok_s, w_s = e_flat[order], tok_flat[order], wts.reshape(-1)[order]
    counts = jnp.zeros((N_EXPERTS,), jnp.int32).at[e_flat].add(1)
    padded = (counts + MOE_BLOCK - 1) // MOE_BLOCK * MOE_BLOCK
    start = jnp.cumsum(counts) - counts
    pend = jnp.cumsum(padded)
    pstart = pend - padded
    dest = pstart[e_s] + jnp.arange(n_asg, dtype=jnp.int32) - start[e_s]
    n_blocks = -(-(n_asg + N_EXPERTS * (MOE_BLOCK - 1)) // MOE_BLOCK)
    n_rows = n_blocks * MOE_BLOCK
    row_tok = jnp.full((n_rows,), n_tok, jnp.int32).at[dest].set(tok_s)
    row_w = jnp.zeros((n_rows,), jnp.float32).at[dest].set(w_s)
    block_e = jnp.minimum(jnp.searchsorted(pend, jnp.arange(n_blocks) * MOE_BLOCK, side="right"), N_EXPERTS - 1)
    t_pad = jnp.concatenate([t, jnp.zeros((1, d), t.dtype)], 0)
    xb = t_pad[row_tok].reshape(n_blocks, MOE_BLOCK, d)

    def expert_block(args):
        xblk, e = args
        hid = jax.nn.silu(xblk @ w_gate[e]) * (xblk @ w_up[e])
        return hid @ w_down[e]

    yb = lax.map(expert_block, (xb, block_e)).reshape(n_rows, d)
    y = jnp.zeros((n_tok + 1, d), jnp.float32).at[row_tok].add(yb.astype(jnp.float32) * row_w[:, None])
    return y[:n_tok].astype(t.dtype)


def setup_inputs(seed: int = 0) -> dict:
    key = jax.random.key(seed)
    keys = jax.random.split(key, 40)
    ctr = [0]
    f32 = jnp.float32

    def nxt():
        ctr[0] += 1
        return keys[ctr[0] - 1]

    def nrm(shape, scale=1.0):
        return jax.random.normal(nxt(), shape, f32) * scale

    d = D_MODEL
    n_a = len(range(0, DEPTH, N_MIXERS))
    n_b = len(range(1, DEPTH, N_MIXERS))
    n_c = len(range(2, DEPTH, N_MIXERS))
    a_cols = (A_HEADS + 2 * A_KV_HEADS) * A_HEAD_DIM
    a_w = A_HEADS * A_HEAD_DIM
    cw = C_HEADS * C_HEAD_DIM
    dt = jnp.exp(jax.random.uniform(nxt(), (n_c, 2, C_HEADS), f32, math.log(1e-3), math.log(1e-1)))
    return {
        "x": nrm((BATCH, SEQ, d)),
        "c": nrm((BATCH, d)),
        "ctx": nrm((BATCH, CTX_LEN, d)),
        "c_ctx": nrm((d,)),
        "w_ada": nrm((DEPTH, d, 6 * d), d ** -0.5),
        "b_ada": nrm((DEPTH, 6 * d), 0.02),
        "ln_g": 1.0 + nrm((DEPTH, 2, d), 0.02),
        "ln_b": nrm((DEPTH, 2, d), 0.02),
        "a_wqkv": nrm((n_a, d, a_cols), d ** -0.5),
        "a_qg": 1.0 + nrm((n_a, A_HEAD_DIM), 0.02),
        "a_kg": 1.0 + nrm((n_a, A_HEAD_DIM), 0.02),
        "a_wo": nrm((n_a, a_w, d), a_w ** -0.5 * DN_BETA),
        "b_win": nrm((n_b, d, 2 * B_WIDTH), d ** -0.5),
        "b_vg": 1.0 + nrm((n_b, B_WIDTH), 0.02),
        "b_vb": nrm((n_b, B_WIDTH), 0.02),
        "b_ws": nrm((n_b, B_GROUPS, B_CHUNK, B_CHUNK), B_CHUNK ** -0.5),
        "b_bs": 1.0 + nrm((n_b, B_GROUPS, B_CHUNK), 0.02),
        "b_wout": nrm((n_b, B_WIDTH, d), B_WIDTH ** -0.5 * DN_BETA),
        "c_win": nrm((n_c, d, 4 * cw + 4 * C_HEADS), d ** -0.5),
        "c_conv": nrm((n_c, C_CONV, 3 * cw), C_CONV ** -0.5),
        "c_alog": jnp.log(jax.random.uniform(nxt(), (n_c, 2, C_HEADS), f32, 1.0, 16.0)),
        "c_dtb": dt + jnp.log(-jnp.expm1(-dt)),
        "c_ng": 1.0 + nrm((n_c, C_HEAD_DIM), 0.02),
        "c_wout": nrm((n_c, cw, d), cw ** -0.5 * DN_BETA),
        "w_router": nrm((d, N_EXPERTS), d ** -0.5),
        "router_bias": nrm((N_EXPERTS,), 0.01),
        "e_gate": nrm((DEPTH, N_EXPERTS, d, D_EXPERT), d ** -0.5),
        "e_up": nrm((DEPTH, N_EXPERTS, d, D_EXPERT), d ** -0.5),
        "e_down": nrm((DEPTH, N_EXPERTS, D_EXPERT, d), D_EXPERT ** -0.5 * DN_BETA),
    }


def reference(x, c, ctx, c_ctx, w_ada, b_ada, ln_g, ln_b,
              a_wqkv, a_qg, a_kg, a_wo,
              b_win, b_vg, b_vb, b_ws, b_bs, b_wout,
              c_win, c_conv, c_alog, c_dtb, c_ng, c_wout,
              w_router, router_bias, e_gate, e_up, e_down):
    d = x.shape[-1]
    cos, sin = axial_rope(x.shape[1])
    s_lat = jax.nn.silu(c)
    s_ctx = jax.nn.silu(c_ctx)[None]
    h, hc = x, ctx
    for i in range(DEPTH):
        last = i == DEPTH - 1
        m, j = i % N_MIXERS, i // N_MIXERS
        sh1, sc1, g1, sh2, sc2, g2 = jnp.split(s_lat @ w_ada[i] + b_ada[i], 6, -1)
        csh1, csc1, cg1, csh2, csc2, cg2 = jnp.split(s_ctx @ w_ada[i] + b_ada[i], 6, -1)
        u = modulate(h, sh1, sc1)
        uc = modulate(hc, csh1, csc1)
        if m == 0:
            y, yc = attention_mixer(u, uc, a_wqkv[j], a_qg[j], a_kg[j], a_wo[j], cos, sin, not last)
        elif m == 1:
            y = chunk_mlp(u, b_win[j], b_vg[j], b_vb[j], b_ws[j], b_bs[j], b_wout[j])
            yc = None if last else chunk_mlp(uc, b_win[j], b_vg[j], b_vb[j], b_ws[j], b_bs[j], b_wout[j])
        else:
            y, yc = deltanet_mixer(u, uc, c_win[j], c_conv[j], c_alog[j], c_dtb[j], c_ng[j], c_wout[j], not last)
        h = layer_norm(DN_ALPHA * h + g1[:, None, :] * y, ln_g[i, 0], ln_b[i, 0])
        if not last:
            hc = layer_norm(DN_ALPHA * hc + cg1[:, None, :] * yc, ln_g[i, 0], ln_b[i, 0])
        u = modulate(h, sh2, sc2)
        if last:
            y = moe_ffn(u.reshape(-1, d), w_router, router_bias, e_gate[i], e_up[i], e_down[i]).reshape(h.shape)
            h = layer_norm(DN_ALPHA * h + g2[:, None, :] * y, ln_g[i, 1], ln_b[i, 1])
        else:
            uc = modulate(hc, csh2, csc2)
            n_lat_tok = u.shape[0] * u.shape[1]
            yt = moe_ffn(jnp.concatenate([u.reshape(-1, d), uc.reshape(-1, d)], 0),
                         w_router, router_bias, e_gate[i], e_up[i], e_down[i])
            y = yt[:n_lat_tok].reshape(h.shape)
            yc = yt[n_lat_tok:].reshape(hc.shape)
            h = layer_norm(DN_ALPHA * h + g2[:, None, :] * y, ln_g[i, 1], ln_b[i, 1])
            hc = layer_norm(DN_ALPHA * hc + cg2[:, None, :] * yc, ln_g[i, 1], ln_b[i, 1])
    return h
```

```python
import functools
import math

import jax
import jax.numpy as jnp
from jax import lax
from jax.experimental import pallas as pl
from jax.experimental.pallas import tpu as pltpu

F32 = jnp.float32
BF16 = jnp.bfloat16

GRID_W = 64
A_HEADS = 8
A_KV_HEADS = 2
A_GROUP = A_HEADS // A_KV_HEADS
HEAD_DIM = 128
ROPE_THETA = 10000.0
B_CHUNK = 128
B_GROUPS = 8
C_HEADS = 8
C_CONV = 3
C_CHUNK = 64
N_EXPERTS = 16
N_GROUPS = 4
TOP_K = 2
EPS = 1e-6
N_MIXERS = 3

LANES = 128
SUBLANES = 8
VMEM_LIMIT_BYTES = 56 * 1024 * 1024
LOG2E = 1.4426950408889634

MOE_ROWS = 512


def _cparams(sem, vmem=VMEM_LIMIT_BYTES):
    return pltpu.CompilerParams(dimension_semantics=sem, vmem_limit_bytes=vmem)


def _pick_tile(cap, *sizes):
    t = cap
    while t > SUBLANES and any(s % t for s in sizes):
        t //= 2
    assert all(s % t == 0 for s in sizes), (cap, sizes)
    return t


def _layer_norm(x, g, b):
    mu = jnp.mean(x, -1, keepdims=True)
    xc = x - mu
    var = jnp.mean(xc * xc, -1, keepdims=True)
    return xc * lax.rsqrt(var + EPS) * g + b


def _silu(x):
    return x * jax.nn.sigmoid(x)


def _bdot(a, b):
    return jnp.dot(a.astype(BF16), b.astype(BF16), preferred_element_type=F32)


def _bdot_nt(a, b):
    return lax.dot_general(a.astype(BF16), b.astype(BF16), (((1,), (1,)), ((), ())), preferred_element_type=F32)


def _split3(x):
    x1 = x.astype(BF16)
    r = x - x1.astype(F32)
    x2 = r.astype(BF16)
    x3 = (r - x2.astype(F32)).astype(BF16)
    return x1, x2, x3


def _ada_kernel(c_ref, w_ref, b_ref, o_ref):
    s = _silu(c_ref[...])
    o_ref[...] = jnp.dot(s, w_ref[...], preferred_element_type=F32, precision=lax.Precision.HIGHEST) + b_ref[...]


def _ada_table(cond, w_ada, b_ada):
    depth, d, d6 = w_ada.shape
    tn = _pick_tile(1536, d6)
    return pl.pallas_call(
        _ada_kernel,
        grid=(depth, d6 // tn),
        in_specs=[
            pl.BlockSpec((SUBLANES, d), lambda i, j: (0, 0)),
            pl.BlockSpec((None, d, tn), lambda i, j: (i, 0, j)),
            pl.BlockSpec((None, 1, tn), lambda i, j: (i, 0, j)),
        ],
        out_specs=pl.BlockSpec((None, SUBLANES, tn), lambda i, j: (i, 0, j)),
        out_shape=jax.ShapeDtypeStruct((depth, SUBLANES, d6), F32),
        compiler_params=_cparams(("arbitrary", "arbitrary")),
        name="ada_table",
    )(cond, w_ada, b_ada.reshape(depth, 1, d6))


SH1, SC1, G1, SH2, SC2, G2 = range(6)


def _mod_spec(d, blocks_per_batch, batch):
    return pl.BlockSpec((None, SUBLANES, d), lambda i, *_: (jnp.minimum(i // blocks_per_batch, batch), 0, 0))


def _modulate(h, m_ref, sh, sc):
    return h * (1.0 + m_ref[sc:sc + 1, :]) + m_ref[sh:sh + 1, :]


def _residual_ln(h, y, m_ref, gate, lng_ref, lnb_ref, alpha):
    return _layer_norm(alpha * h + m_ref[gate:gate + 1, :] * y, lng_ref[...], lnb_ref[...])


def _attn_proj_kernel(h_ref, m_ref, w_ref, qg_ref, kg_ref, cos_ref, sin_ref, q_ref, k_ref, v_ref):
    u = _modulate(h_ref[...], m_ref, SH1, SC1)
    qkv = _bdot(u, w_ref[...])
    cos = cos_ref[...]
    sin = sin_ref[...]
    nq = A_HEADS * HEAD_DIM
    nk = A_KV_HEADS * HEAD_DIM

    def norm_rope(xh, g):
        xn = xh * lax.rsqrt(jnp.mean(xh * xh, -1, keepdims=True) + EPS) * g
        return xn * cos + pltpu.roll(xn, HEAD_DIM // 2, 1) * sin

    for hd in range(A_HEADS):
        sl = slice(hd * HEAD_DIM, (hd + 1) * HEAD_DIM)
        q_ref[:, sl] = norm_rope(qkv[:, sl], qg_ref[...]).astype(BF16)
    for hd in range(A_KV_HEADS):
        sl = slice(hd * HEAD_DIM, (hd + 1) * HEAD_DIM)
        k_ref[:, sl] = norm_rope(qkv[:, nq + hd * HEAD_DIM:nq + (hd + 1) * HEAD_DIM], kg_ref[...]).astype(BF16)
    v_ref[...] = qkv[:, nq + nk:].astype(BF16)


def _attn_proj(h, mod, w_qkv, qg, kg, cos_t, sin_t, tm, n_lat, batch):
    t, d = h.shape
    nq = A_HEADS * HEAD_DIM
    nk = A_KV_HEADS * HEAD_DIM
    bpb = n_lat // tm

    def rope_map(i):
        return (jnp.where(i < batch * bpb, i % bpb, bpb), 0)

    return pl.pallas_call(
        _attn_proj_kernel,
        grid=(t // tm,),
        in_specs=[
            pl.BlockSpec((tm, d), lambda i: (i, 0)),
            _mod_spec(d, bpb, batch),
            pl.BlockSpec((d, nq + 2 * nk), lambda i: (0, 0)),
            pl.BlockSpec((1, HEAD_DIM), lambda i: (0, 0)),
            pl.BlockSpec((1, HEAD_DIM), lambda i: (0, 0)),
            pl.BlockSpec((tm, HEAD_DIM), rope_map),
            pl.BlockSpec((tm, HEAD_DIM), rope_map),
        ],
        out_specs=[
            pl.BlockSpec((tm, nq), lambda i: (i, 0)),
            pl.BlockSpec((tm, nk), lambda i: (i, 0)),
            pl.BlockSpec((tm, nk), lambda i: (i, 0)),
        ],
        out_shape=[
            jax.ShapeDtypeStruct((t, nq), BF16),
            jax.ShapeDtypeStruct((t, nk), BF16),
            jax.ShapeDtypeStruct((t, nk), BF16),
        ],
        compiler_params=_cparams(("parallel",)),
        name="attn_proj",
    )(h, mod, w_qkv, qg, kg, cos_t, sin_t)


def _flash_kernel(*refs, seg_chunks, tq):
    nseg = len(seg_chunks)
    q_ref = refs[0]
    kv_refs = refs[1:1 + 2 * nseg]
    o_ref = refs[1 + 2 * nseg]
    m_ref, l_ref, acc_ref = refs[2 + 2 * nseg:]
    q = jnp.concatenate([q_ref[:, g * HEAD_DIM:(g + 1) * HEAD_DIM] for g in range(A_GROUP)], axis=0)
    m_ref[...] = jnp.full(m_ref.shape, -jnp.inf, F32)
    l_ref[...] = jnp.zeros(l_ref.shape, F32)
    acc_ref[...] = jnp.zeros(acc_ref.shape, F32)

    for s, (nchunk, tk) in enumerate(seg_chunks):
        k_ref, v_ref = kv_refs[2 * s], kv_refs[2 * s + 1]

        def body(j, carry, k_ref=k_ref, v_ref=v_ref, tk=tk):
            off = pl.multiple_of(j * tk, tk)
            kj = k_ref[pl.ds(off, tk), :]
            vj = v_ref[pl.ds(off, tk), :]
            sc = lax.dot_general(q, kj, (((1,), (1,)), ((), ())), preferred_element_type=F32)
            m_old = m_ref[...]
            m_new = jnp.maximum(m_old, jnp.max(sc, -1, keepdims=True))
            alpha = jnp.exp2(m_old - m_new)
            p = jnp.exp2(sc - m_new)
            l_ref[...] = alpha * l_ref[...] + jnp.sum(p, -1, keepdims=True)
            acc_ref[...] = alpha * acc_ref[...] + jnp.dot(p.astype(BF16), vj, preferred_element_type=F32)
            m_ref[...] = m_new
            return carry

        lax.fori_loop(0, nchunk, body, 0)

    out = acc_ref[...] / l_ref[...]
    for g in range(A_GROUP):
        o_ref[:, g * HEAD_DIM:(g + 1) * HEAD_DIM] = out[g * tq:(g + 1) * tq].astype(o_ref.dtype)


def _key_tile(n):
    for tk in (1024, 768, 512, 384, 256, 128):
        if n % tk == 0:
            return tk
    raise ValueError(n)


def _flash(q, k, v, *, batch, q_row0, q_len, segs, tq):
    gw = A_GROUP * HEAD_DIM
    nqb = q_len // tq
    in_specs = [pl.BlockSpec((tq, gw), lambda b, kv, i: (q_row0 // tq + b * nqb + i, kv))]
    args = [q]
    seg_chunks = []
    for row0, ln in segs:
        assert row0 % ln == 0
        tk = _key_tile(ln)
        seg_chunks.append((ln // tk, tk))
        for arr in (k, v):
            in_specs.append(pl.BlockSpec((ln, HEAD_DIM), lambda b, kv, i, r=row0 // ln: (r + b, kv)))
            args.append(arr)
    return pl.pallas_call(
        functools.partial(_flash_kernel, seg_chunks=tuple(seg_chunks), tq=tq),
        grid=(batch, A_KV_HEADS, nqb),
        in_specs=in_specs,
        out_specs=pl.BlockSpec((tq, gw), lambda b, kv, i: (b * nqb + i, kv)),
        out_shape=jax.ShapeDtypeStruct((batch * q_len, A_HEADS * HEAD_DIM), BF16),
        scratch_shapes=[
            pltpu.VMEM((A_GROUP * tq, 1), F32),
            pltpu.VMEM((A_GROUP * tq, 1), F32),
            pltpu.VMEM((A_GROUP * tq, HEAD_DIM), F32),
        ],
        compiler_params=_cparams(("parallel", "parallel", "arbitrary")),
        name="flash_attn",
    )(*args)


def _out_proj_kernel(a_ref, w_ref, h_ref, m_ref, lng_ref, lnb_ref, o_ref, *, alpha):
    y = jnp.dot(a_ref[...], w_ref[...], preferred_element_type=F32)
    o_ref[...] = _residual_ln(h_ref[...], y, m_ref, G1, lng_ref, lnb_ref, alpha)


def _out_proj(a, w, h, mod, lng, lnb, tm, n_lat, batch, alpha):
    t, d = h.shape
    ka = a.shape[1]
    return pl.pallas_call(
        functools.partial(_out_proj_kernel, alpha=alpha),
        grid=(t // tm,),
        in_specs=[
            pl.BlockSpec((tm, ka), lambda i: (i, 0)),
            pl.BlockSpec((ka, d), lambda i: (0, 0)),
            pl.BlockSpec((tm, d), lambda i: (i, 0)),
            _mod_spec(d, n_lat // tm, batch),
            pl.BlockSpec((1, d), lambda i: (0, 0)),
            pl.BlockSpec((1, d), lambda i: (0, 0)),
        ],
        out_specs=pl.BlockSpec((tm, d), lambda i: (i, 0)),
        out_shape=jax.ShapeDtypeStruct((t, d), F32),
        compiler_params=_cparams(("parallel",)),
        name="out_proj_ln",
    )(a, w, h, mod, lng, lnb)


def _rope_tables(n, tm):
    rows = n // GRID_W
    half = HEAD_DIM // 2
    inv = 1.0 / (ROPE_THETA ** (jnp.arange(0, half, 2, dtype=F32) / half))
    r = jnp.broadcast_to(jnp.arange(rows, dtype=F32)[:, None], (rows, GRID_W)).reshape(-1)
    col = jnp.broadcast_to(jnp.arange(GRID_W, dtype=F32)[None, :], (rows, GRID_W)).reshape(-1)
    ang = jnp.concatenate([r[:, None] * inv, col[:, None] * inv], -1)
    cos, sin = jnp.cos(ang), jnp.sin(ang)
    cos_t = jnp.concatenate([jnp.concatenate([cos, cos], -1), jnp.ones((tm, HEAD_DIM), F32)], 0)
    sin_t = jnp.concatenate([jnp.concatenate([-sin, sin], -1), jnp.zeros((tm, HEAD_DIM), F32)], 0)
    return cos_t, sin_t


def _deinterleave_heads(w, n_heads):
    lead = w.shape[:-1]
    w = w.reshape(*lead, n_heads, HEAD_DIM // 2, 2)
    return jnp.swapaxes(w, -1, -2).reshape(*lead, n_heads * HEAD_DIM)


def _attention_layer(h, mod, w_qkv, q_g, k_g, w_o, lng, lnb, rope, dims, alpha):
    batch, n_lat, n_ctx, tm = dims
    nq = A_HEADS * HEAD_DIM
    nk = A_KV_HEADS * HEAD_DIM
    w = jnp.concatenate([
        _deinterleave_heads(w_qkv[:, :nq], A_HEADS),
        _deinterleave_heads(w_qkv[:, nq:nq + nk], A_KV_HEADS),
        w_qkv[:, nq + nk:]], -1).astype(BF16)
    qg = _deinterleave_heads(q_g[None, :], 1) * (HEAD_DIM ** -0.5 * LOG2E)
    kg = _deinterleave_heads(k_g[None, :], 1)
    q, k, v = _attn_proj(h, mod, w, qg, kg, rope[0], rope[1], tm, n_lat, batch)
    lat0 = batch * n_lat
    o_lat = _flash(q, k, v, batch=batch, q_row0=0, q_len=n_lat, segs=[(lat0, n_ctx), (0, n_lat)],
                   tq=_pick_tile(256, n_lat))
    o_ctx = _flash(q, k, v, batch=batch, q_row0=lat0, q_len=n_ctx, segs=[(lat0, n_ctx)],
                   tq=_pick_tile(256, n_ctx))
    o = jnp.concatenate([o_lat, o_ctx], 0)
    return _out_proj(o, w_o.astype(BF16), h, mod, lng, lnb, tm, n_lat, batch, alpha)


def _gmlp_kernel(h_ref, m_ref, win_ref, vg_ref, vb_ref, ws_ref, bs_ref, wout_ref, lng_ref, lnb_ref, o_ref,
                 gated_ref, *, alpha, width):
    h = h_ref[...]
    u = _modulate(h, m_ref, SH1, SC1)
    z = jax.nn.gelu(_bdot(u, win_ref[...]))
    uu = z[:, :width]
    v = _layer_norm(z[:, width:], vg_ref[...], vb_ref[...]).astype(BF16)
    tm = h.shape[0]
    gw = width // B_GROUPS
    for c in range(tm // B_CHUNK):
        rs = slice(c * B_CHUNK, (c + 1) * B_CHUNK)
        for g in range(B_GROUPS):
            cs = slice(g * gw, (g + 1) * gw)
            mixed = jnp.dot(ws_ref[g], v[rs, cs], preferred_element_type=F32) + bs_ref[g]
            gated_ref[rs, cs] = (uu[rs, cs] * mixed).astype(BF16)
    y = jnp.dot(gated_ref[...], wout_ref[...], preferred_element_type=F32)
    o_ref[...] = _residual_ln(h, y, m_ref, G1, lng_ref, lnb_ref, alpha)


def _gmlp_layer(h, mod, w_in, v_g, v_b, w_s, b_s, w_out, lng, lnb, dims, alpha):
    batch, n_lat, n_ctx, tm = dims
    t, d = h.shape
    width = w_out.shape[0]
    gw = width // B_GROUPS
    assert tm % B_CHUNK == 0 and n_ctx % B_CHUNK == 0
    bs = jnp.broadcast_to(b_s[:, :, None], (B_GROUPS, B_CHUNK, gw)).astype(F32)
    const = lambda *shape: pl.BlockSpec(shape, lambda i: (0,) * len(shape))
    return pl.pallas_call(
        functools.partial(_gmlp_kernel, alpha=alpha, width=width),
        grid=(t // tm,),
        in_specs=[
            pl.BlockSpec((tm, d), lambda i: (i, 0)),
            _mod_spec(d, n_lat // tm, batch),
            const(d, 2 * width), const(1, width), const(1, width),
            const(B_GROUPS, B_CHUNK, B_CHUNK), const(B_GROUPS, B_CHUNK, gw),
            const(width, d), const(1, d), const(1, d),
        ],
        out_specs=pl.BlockSpec((tm, d), lambda i: (i, 0)),
        out_shape=jax.ShapeDtypeStruct((t, d), F32),
        scratch_shapes=[pltpu.VMEM((tm, width), BF16)],
        compiler_params=_cparams(("parallel",)),
        name="gmlp_layer",
    )(h, mod, w_in.astype(BF16), v_g[None], v_b[None], w_s.astype(BF16), bs, w_out.astype(BF16), lng, lnb)


def _delta_proj_kernel(h_ref, m_ref, w_ref, o_ref):
    u = _modulate(h_ref[...], m_ref, SH1, SC1)
    o_ref[...] = _bdot(u, w_ref[...])


def _delta_proj(h, mod, w, tm, n_lat, batch):
    t, d = h.shape
    ncol = w.shape[1]
    tn = _pick_tile(1024, ncol - LANES)
    nj = (ncol - LANES) // tn
    return pl.pallas_call(
        _delta_proj_kernel,
        grid=(t // tm, nj),
        in_specs=[
            pl.BlockSpec((tm, d), lambda i, j: (i, 0)),
            pl.BlockSpec((None, SUBLANES, d), lambda i, j: (jnp.minimum(i // (n_lat // tm), batch), 0, 0)),
            pl.BlockSpec((d, tn), lambda i, j: (0, j)),
        ],
        out_specs=pl.BlockSpec((tm, tn), lambda i, j: (i, j)),
        out_shape=jax.ShapeDtypeStruct((t, ncol - LANES), F32),
        compiler_params=_cparams(("parallel", "arbitrary")),
        name="delta_proj",
    )(h, mod, w[:, :ncol - LANES]), pl.pallas_call(
        _delta_proj_kernel,
        grid=(t // tm, 1),
        in_specs=[
            pl.BlockSpec((tm, d), lambda i, j: (i, 0)),
            pl.BlockSpec((None, SUBLANES, d), lambda i, j: (jnp.minimum(i // (n_lat // tm), batch), 0, 0)),
            pl.BlockSpec((d, LANES), lambda i, j: (0, 0)),
        ],
        out_specs=pl.BlockSpec((tm, LANES), lambda i, j: (i, 0)),
        out_shape=jax.ShapeDtypeStruct((t, LANES), F32),
        compiler_params=_cparams(("parallel", "arbitrary")),
        name="delta_proj_gates",
    )(h, mod, w[:, ncol - LANES:])


def _delta_prep_kernel(x_ref, xp_ref, xn_ref, ab_ref, cw_ref, ga_ref, dtb_ref, q_ref, k_ref, v_ref, gb_ref,
                       *, tm, n_lat_rows, n_lat, n_ctx, cw):
    i = pl.program_id(0)
    row0 = i * tm
    in_lat = row0 < n_lat_rows
    pos = jnp.where(in_lat, row0 % n_lat, (row0 - n_lat_rows) % n_ctx)
    seq = jnp.where(in_lat, n_lat, n_ctx)
    has_prev = (pos != 0).astype(F32)
    has_next = (pos + tm != seq).astype(F32)
    rows = lax.broadcasted_iota(jnp.int32, (tm, HEAD_DIM), 0)
    first = rows == 0
    last = rows == tm - 1
    heads = cw // HEAD_DIM
    for part, out_ref in enumerate((q_ref, k_ref, v_ref)):
        for hd in range(heads):
            sl = slice(part * cw + hd * HEAD_DIM, part * cw + (hd + 1) * HEAD_DIM)
            x = x_ref[:, sl]
            prev_row = xp_ref[SUBLANES - 1:SUBLANES, sl] * has_prev
            next_row = xn_ref[0:1, sl] * has_next
            x_prev = jnp.where(first, prev_row, pltpu.roll(x, 1, 0))
            x_next = jnp.where(last, next_row, pltpu.roll(x, tm - 1, 0))
            y = _silu(x_prev * cw_ref[0:1, sl] + x * cw_ref[1:2, sl] + x_next * cw_ref[2:3, sl])
            if part < 2:
                y = y * lax.rsqrt(jnp.sum(y * y, -1, keepdims=True) + EPS)
            if part == 0:
                y = y * (HEAD_DIM ** -0.5)
            out_ref[:, hd * HEAD_DIM:(hd + 1) * HEAD_DIM] = y.astype(out_ref.dtype)
    ab = ab_ref[...]
    xg = ab + dtb_ref[...]
    softplus = jnp.maximum(xg, 0.0) + jnp.log1p(jnp.exp(-jnp.abs(xg)))
    lane = lax.broadcasted_iota(jnp.int32, ab.shape, 1)
    gb_ref[...] = jnp.where(lane < 2 * C_HEADS, ga_ref[...] * softplus, jax.nn.sigmoid(ab))


def _delta_prep(proj, ab, conv_w, a_log, dt_bias, tm, n_lat, n_ctx, batch):
    t = proj.shape[0]
    cw = C_HEADS * HEAD_DIM
    r8 = tm // SUBLANES
    nb8 = t // SUBLANES
    ga = jnp.zeros((1, LANES), F32).at[0, :2 * C_HEADS].set(-jnp.exp(a_log.astype(F32)).reshape(-1))
    dtb = jnp.zeros((1, LANES), F32).at[0, :2 * C_HEADS].set(dt_bias.astype(F32).reshape(-1))
    cwp = jnp.zeros((SUBLANES, 3 * cw), F32).at[:C_CONV].set(conv_w)
    return pl.pallas_call(
        functools.partial(_delta_prep_kernel, tm=tm, n_lat_rows=batch * n_lat, n_lat=n_lat, n_ctx=n_ctx, cw=cw),
        grid=(t // tm,),
        in_specs=[
            pl.BlockSpec((tm, 3 * cw), lambda i: (i, 0)),
            pl.BlockSpec((SUBLANES, 3 * cw), lambda i: (jnp.maximum(i * r8 - 1, 0), 0)),
            pl.BlockSpec((SUBLANES, 3 * cw), lambda i: (jnp.minimum((i + 1) * r8, nb8 - 1), 0)),
            pl.BlockSpec((tm, LANES), lambda i: (i, 0)),
            pl.BlockSpec((SUBLANES, 3 * cw), lambda i: (0, 0)),
            pl.BlockSpec((1, LANES), lambda i: (0, 0)),
            pl.BlockSpec((1, LANES), lambda i: (0, 0)),
        ],
        out_specs=[pl.BlockSpec((tm, cw), lambda i: (i, 0))] * 3 + [pl.BlockSpec((tm, LANES), lambda i: (i, 0))],
        out_shape=[jax.ShapeDtypeStruct((t, cw), BF16)] * 3 + [jax.ShapeDtypeStruct((t, LANES), F32)],
        compiler_params=_cparams(("parallel",)),
        name="delta_prep",
    )(proj, proj, proj, ab, cwp, ga, dtb)


def _unit_lower_inverse(a, blk_mask, eye):
    n = jnp.where(blk_mask, a, 0.0)
    off = a - n
    n2 = _bdot(n, n)
    n4 = _bdot(n2, n2)
    n8 = _bdot(n4, n4)
    dinv = eye - n + n2 - _bdot(n, n2)
    dinv = dinv + _bdot(dinv, n4)
    dinv = dinv + _bdot(dinv, n8)
    m = _bdot(dinv, off)
    m2 = _bdot(m, m)
    t = eye - m + m2 - _bdot(m, m2)
    return _bdot(t, dinv)


def _delta_scan_kernel(q_ref, k_ref, v_ref, gb_ref, o_ref, s_ref, *, direction):
    c = pl.program_id(1)

    @pl.when(c == 0)
    def _():
        s_ref[...] = jnp.zeros(s_ref.shape, F32)

    L = C_CHUNK
    ri = lax.broadcasted_iota(jnp.int32, (L, L), 0)
    ci = lax.broadcasted_iota(jnp.int32, (L, L), 1)
    if direction == 0:
        tri, strict = ri >= ci, ri > ci
    else:
        tri, strict = ri <= ci, ri < ci
    eye = (ri == ci).astype(F32)
    blk_mask = (ri // 16) == (ci // 16)
    tri_b = tri.astype(BF16)

    gb = gb_ref[...]
    g1, g2, g3 = _split3(gb)
    gc = (jnp.dot(tri_b, g1, preferred_element_type=F32) + jnp.dot(tri_b, g2, preferred_element_type=F32)
          + jnp.dot(tri_b, g3, preferred_element_type=F32))
    gct = gc.T
    last = L - 1 if direction == 0 else 0

    for hd in range(C_HEADS):
        lane = direction * C_HEADS + hd
        sl = slice(hd * HEAD_DIM, (hd + 1) * HEAD_DIM)
        qh, kh, vh = q_ref[:, sl], k_ref[:, sl], v_ref[:, sl]
        kf, qf, vf = kh.astype(F32), qh.astype(F32), vh.astype(F32)
        g_col = gc[:, lane:lane + 1]
        g_row = gct[lane:lane + 1, :]
        beta = gb[:, 2 * C_HEADS + lane:2 * C_HEADS + lane + 1]
        g_last = gc[last:last + 1, lane:lane + 1]
        decay = jnp.exp(jnp.where(tri, g_col - g_row, -jnp.inf))
        kq = _bdot_nt(jnp.concatenate([kh, qh], 0), kh)
        a_mat = jnp.where(strict, kq[:L] * beta * decay, 0.0)
        attn = kq[L:] * decay
        t_inv = _unit_lower_inverse(a_mat, blk_mask, eye)
        eg = jnp.exp(g_col)
        rhs = jnp.concatenate([vf * beta, kf * (beta * eg)], 1)
        uw = _bdot(t_inv, rhs)
        u_i, w_i = uw[:, :HEAD_DIM], uw[:, HEAD_DIM:]
        q_dec = qf * eg
        k_dec = kf * jnp.exp(g_last - g_col)
        s = s_ref[hd]
        ws = _bdot(jnp.concatenate([w_i, q_dec], 0), s)
        v_new = u_i - ws[:L]
        o = ws[L:] + _bdot(attn, v_new)
        s_ref[hd] = s * jnp.exp(g_last) + lax.dot_general(
            k_dec.astype(BF16), v_new.astype(BF16), (((0,), (0,)), ((), ())), preferred_element_type=F32)
        o_ref[:, sl] = o.astype(o_ref.dtype)


def _delta_scan(q, k, v, gb, direction, n_lat, n_ctx, batch):
    t, cw = q.shape
    nlc, ncc = n_lat // C_CHUNK, n_ctx // C_CHUNK
    lat_blocks = batch * nlc

    def row_map(b, c):
        if direction == 0:
            return jnp.where(c < ncc, lat_blocks + b * ncc + c, b * nlc + (c - ncc))
        return jnp.where(c < ncc, lat_blocks + b * ncc + (ncc - 1 - c), b * nlc + (nlc - 1 - (c - ncc)))

    spec = lambda w: pl.BlockSpec((C_CHUNK, w), lambda b, c: (row_map(b, c), 0))
    return pl.pallas_call(
        functools.partial(_delta_scan_kernel, direction=direction),
        grid=(batch, ncc + nlc),
        in_specs=[spec(cw), spec(cw), spec(cw), spec(LANES)],
        out_specs=spec(cw),
        out_shape=jax.ShapeDtypeStruct((t, cw), F32),
        scratch_shapes=[pltpu.VMEM((C_HEADS, HEAD_DIM, HEAD_DIM), F32)],
        compiler_params=_cparams(("parallel", "arbitrary")),
        name=f"delta_scan_{direction}",
    )(q, k, v, gb)


def _delta_out_kernel(of_ref, ob_ref, z_ref, ng_ref, w_ref, h_ref, m_ref, lng_ref, lnb_ref, o_ref, y_ref, *, alpha):
    for hd in range(C_HEADS):
        sl = slice(hd * HEAD_DIM, (hd + 1) * HEAD_DIM)
        o = of_ref[:, sl] + ob_ref[:, sl]
        on = o * lax.rsqrt(jnp.mean(o * o, -1, keepdims=True) + EPS) * ng_ref[...]
        y_ref[:, sl] = (on * _silu(z_ref[:, sl])).astype(BF16)
    y = jnp.dot(y_ref[...], w_ref[...], preferred_element_type=F32)
    o_ref[...] = _residual_ln(h_ref[...], y, m_ref, G1, lng_ref, lnb_ref, alpha)


def _delta_out(o_f, o_b, proj, norm_g, w_out, h, mod, lng, lnb, tm, n_lat, batch, alpha):
    t, d = h.shape
    cw = C_HEADS * HEAD_DIM
    row = lambda w: pl.BlockSpec((tm, w), lambda i: (i, 0))
    return pl.pallas_call(
        functools.partial(_delta_out_kernel, alpha=alpha),
        grid=(t // tm,),
        in_specs=[
            row(cw), row(cw),
            pl.BlockSpec((tm, cw), lambda i: (i, 3)),
            pl.BlockSpec((1, HEAD_DIM), lambda i: (0, 0)),
            pl.BlockSpec((cw, d), lambda i: (0, 0)),
            row(d),
            _mod_spec(d, n_lat // tm, batch),
            pl.BlockSpec((1, d), lambda i: (0, 0)),
            pl.BlockSpec((1, d), lambda i: (0, 0)),
        ],
        out_specs=row(d),
        out_shape=jax.ShapeDtypeStruct((t, d), F32),
        scratch_shapes=[pltpu.VMEM((tm, cw), BF16)],
        compiler_params=_cparams(("parallel",)),
        name="delta_out_ln",
    )(o_f, o_b, proj, norm_g[None], w_out, h, mod, lng, lnb)


def _deltanet_layer(h, mod, w_in, conv_w, a_log, dt_bias, norm_g, w_out, lng, lnb, dims, alpha):
    batch, n_lat, n_ctx, tm = dims
    cw = C_HEADS * HEAD_DIM
    d = h.shape[1]
    n_gate = w_in.shape[1] - 4 * cw
    w = jnp.concatenate([w_in, jnp.zeros((d, LANES - n_gate), w_in.dtype)], 1).astype(BF16)
    proj, ab = _delta_proj(h, mod, w, tm, n_lat, batch)
    tmc = _pick_tile(256, n_lat, n_ctx)
    q, k, v, gb = _delta_prep(proj, ab, conv_w, a_log, dt_bias, tmc, n_lat, n_ctx, batch)
    o_f = _delta_scan(q, k, v, gb, 0, n_lat, n_ctx, batch)
    o_b = _delta_scan(q, k, v, gb, 1, n_lat, n_ctx, batch)
    return _delta_out(o_f, o_b, proj, norm_g, w_out.astype(BF16), h, mod, lng, lnb, tm, n_lat, batch, alpha)


def _router_kernel(h_ref, m_ref, wr_ref, bias_ref, u_ref, idx_ref, wt_ref, cnt_ref, carry_ref):
    i = pl.program_id(0)

    @pl.when(i == 0)
    def _():
        carry_ref[...] = jnp.zeros(carry_ref.shape, F32)

    u = _modulate(h_ref[...], m_ref, SH2, SC2)
    u_ref[...] = u.astype(BF16)
    tm = u.shape[0]
    u1, u2, _ = _split3(u)
    w1, w2, _ = _split3(wr_ref[...])
    nt = lambda a, b: lax.dot_general(a, b, (((1,), (1,)), ((), ())), preferred_element_type=F32)
    logits = nt(w1, u1) + nt(w1, u2) + nt(w2, u1)
    scores = jax.nn.sigmoid(logits)
    sel = scores + bias_ref[:, 0:1]
    per = N_EXPERTS // N_GROUPS
    best = None
    gidx = jnp.zeros((1, tm), jnp.int32)
    for g in range(N_GROUPS):
        r = [sel[g * per + j:g * per + j + 1, :] for j in range(per)]
        top2 = None
        for a in range(per):
            for b in range(a + 1, per):
                pair = r[a] + r[b]
                top2 = pair if top2 is None else jnp.maximum(top2, pair)
        if best is None:
            best = top2
        else:
            better = top2 > best
            gidx = jnp.where(better, g, gidx)
            best = jnp.where(better, top2, best)
    eio = lax.broadcasted_iota(jnp.int32, (N_EXPERTS, tm), 0)
    masked = jnp.where(eio // per == gidx, sel, -jnp.inf)
    m1 = jnp.max(masked, 0, keepdims=True)
    i1 = jnp.min(jnp.where(masked == m1, eio, N_EXPERTS), 0, keepdims=True)
    masked2 = jnp.where(eio == i1, -jnp.inf, masked)
    m2 = jnp.max(masked2, 0, keepdims=True)
    i2 = jnp.min(jnp.where(masked2 == m2, eio, N_EXPERTS), 0, keepdims=True)
    oh1 = eio == i1
    oh2 = eio == i2
    s1 = jnp.sum(jnp.where(oh1, scores, 0.0), 0, keepdims=True)
    s2 = jnp.sum(jnp.where(oh2, scores, 0.0), 0, keepdims=True)
    tot = s1 + s2
    oh = jnp.where(oh1 | oh2, 1.0, 0.0)
    ti = lax.broadcasted_iota(jnp.int32, (tm, tm), 0)
    tj = lax.broadcasted_iota(jnp.int32, (tm, tm), 1)
    before = (ti < tj).astype(BF16)
    rank = carry_ref[:, 0:1] + jnp.dot(oh.astype(BF16), before, preferred_element_type=F32)
    r1 = jnp.sum(jnp.where(oh1, rank, 0.0), 0, keepdims=True)
    r2 = jnp.sum(jnp.where(oh2, rank, 0.0), 0, keepdims=True)
    new_carry = carry_ref[...] + jnp.sum(oh, 1, keepdims=True)
    carry_ref[...] = new_carry
    cnt_ref[...] = new_carry
    zi = jnp.zeros((SUBLANES - 4, tm), jnp.int32)
    idx_ref[...] = jnp.concatenate([i1, i2, r1.astype(jnp.int32), r2.astype(jnp.int32), zi], 0)
    wt_ref[...] = jnp.concatenate([s1 / tot, s2 / tot, jnp.zeros((SUBLANES - 2, tm), F32)], 0)


def _router(h, mod, w_router, router_bias, tm, n_lat, batch):
    t, d = h.shape
    wr = w_router.T.astype(F32)
    bias = jnp.broadcast_to(router_bias.astype(F32)[:, None], (N_EXPERTS, LANES))
    return pl.pallas_call(
        _router_kernel,
        grid=(t // tm,),
        in_specs=[
            pl.BlockSpec((tm, d), lambda i: (i, 0)),
            _mod_spec(d, n_lat // tm, batch),
            pl.BlockSpec((N_EXPERTS, d), lambda i: (0, 0)),
            pl.BlockSpec((N_EXPERTS, LANES), lambda i: (0, 0)),
        ],
        out_specs=[
            pl.BlockSpec((tm, d), lambda i: (i, 0)),
            pl.BlockSpec((SUBLANES, tm), lambda i: (0, i)),
            pl.BlockSpec((SUBLANES, tm), lambda i: (0, i)),
            pl.BlockSpec((N_EXPERTS, LANES), lambda i: (0, 0)),
        ],
        out_shape=[
            jax.ShapeDtypeStruct((t, d), BF16),
            jax.ShapeDtypeStruct((SUBLANES, t), jnp.int32),
            jax.ShapeDtypeStruct((SUBLANES, t), F32),
            jax.ShapeDtypeStruct((N_EXPERTS, LANES), F32),
        ],
        scratch_shapes=[pltpu.VMEM((N_EXPERTS, LANES), F32)],
        compiler_params=_cparams(("arbitrary",)),
        name="moe_router",
    )(h, mod, wr, bias)


def _expert_kernel(be_ref, nu_ref, x_ref, wg_ref, wu_ref, wd_ref, o_ref):
    i = pl.program_id(0)

    @pl.when(i < nu_ref[0])
    def _():
        x = x_ref[...]
        hid = _silu(_bdot(x, wg_ref[...])) * _bdot(x, wu_ref[...])
        o_ref[...] = _bdot(hid, wd_ref[...]).astype(o_ref.dtype)

    @pl.when(i >= nu_ref[0])
    def _():
        o_ref[...] = jnp.zeros(o_ref.shape, o_ref.dtype)


def _experts(xs, block_e, n_used, e_gate, e_up, e_down, layer):
    n_rows, d = xs.shape
    de = e_gate.shape[-1]
    nb = n_rows // MOE_ROWS
    grid_spec = pltpu.PrefetchScalarGridSpec(
        num_scalar_prefetch=2,
        grid=(nb,),
        in_specs=[
            pl.BlockSpec((MOE_ROWS, d), lambda i, be, nu: (i, 0)),
            pl.BlockSpec((None, None, d, de), lambda i, be, nu: (layer, be[i], 0, 0)),
            pl.BlockSpec((None, None, d, de), lambda i, be, nu: (layer, be[i], 0, 0)),
            pl.BlockSpec((None, None, de, d), lambda i, be, nu: (layer, be[i], 0, 0)),
        ],
        out_specs=pl.BlockSpec((MOE_ROWS, d), lambda i, be, nu: (i, 0)),
    )
    return pl.pallas_call(
        _expert_kernel,
        grid_spec=grid_spec,
        out_shape=jax.ShapeDtypeStruct((n_rows, d), BF16),
        compiler_params=_cparams(("arbitrary",)),
        name="moe_experts",
    )(block_e, n_used, xs, e_gate, e_up, e_down)


def _moe_combine_kernel(y_ref, h_ref, m_ref, lng_ref, lnb_ref, o_ref, *, alpha):
    o_ref[...] = _residual_ln(h_ref[...], y_ref[...].astype(F32), m_ref, G2, lng_ref, lnb_ref, alpha)


def _moe_combine(y, h, mod, lng, lnb, tm, n_lat, batch, alpha):
    t, d = h.shape
    row = pl.BlockSpec((tm, d), lambda i: (i, 0))
    return pl.pallas_call(
        functools.partial(_moe_combine_kernel, alpha=alpha),
        grid=(t // tm,),
        in_specs=[row, row, _mod_spec(d, n_lat // tm, batch),
                  pl.BlockSpec((1, d), lambda i: (0, 0)), pl.BlockSpec((1, d), lambda i: (0, 0))],
        out_specs=row,
        out_shape=jax.ShapeDtypeStruct((t, d), F32),
        compiler_params=_cparams(("parallel",)),
        name="moe_combine_ln",
    )(y, h, mod, lng, lnb)


def _moe_layer(h, mod, w_router, router_bias, e_gate, e_up, e_down, layer, lng, lnb, dims, alpha):
    batch, n_lat, n_ctx, tm = dims
    t, d = h.shape
    tr = _pick_tile(256, n_lat, batch * n_ctx)
    u, idx, wts, cnt = _router(h, mod, w_router, router_bias, tr, n_lat, batch)
    counts = cnt[:, 0].astype(jnp.int32)
    padded = (counts + MOE_ROWS - 1) // MOE_ROWS * MOE_ROWS
    pend = jnp.cumsum(padded)
    pstart = pend - padded
    nb = -(-(t * TOP_K + N_EXPERTS * (MOE_ROWS - 1)) // MOE_ROWS)
    n_rows = nb * MOE_ROWS
    eidx, rank = idx[0:TOP_K], idx[TOP_K:2 * TOP_K]
    dest = pstart[eidx] + rank
    tok = jnp.broadcast_to(jnp.arange(t, dtype=jnp.int32)[None, :], (TOP_K, t))
    row_tok = jnp.zeros((n_rows,), jnp.int32).at[dest.reshape(-1)].set(tok.reshape(-1), unique_indices=True)
    block_e = jnp.minimum(jnp.searchsorted(pend, jnp.arange(nb, dtype=jnp.int32) * MOE_ROWS, side="right"),
                          N_EXPERTS - 1).astype(jnp.int32)
    n_used = (pend[-1:] // MOE_ROWS).astype(jnp.int32)
    xs = jnp.take(u, row_tok, axis=0)
    yb = _experts(xs, block_e, n_used, e_gate, e_up, e_down, layer)
    w = wts[0:TOP_K]
    y = (jnp.take(yb, dest[0], axis=0).astype(F32) * w[0][:, None]
         + jnp.take(yb, dest[1], axis=0).astype(F32) * w[1][:, None])
    return _moe_combine(y, h, mod, lng, lnb, tm, n_lat, batch, alpha)


def kernel(x, c, ctx, c_ctx, w_ada, b_ada, ln_g, ln_b, a_wqkv, a_qg, a_kg, a_wo, b_win, b_vg, b_vb, b_ws, b_bs,
           b_wout, c_win, c_conv, c_alog, c_dtb, c_ng, c_wout, w_router, router_bias, e_gate, e_up, e_down):
    batch, n_lat, d = x.shape
    n_ctx = ctx.shape[1]
    depth = w_ada.shape[0]
    alpha = (2 * depth) ** 0.25
    assert batch + 1 <= SUBLANES
    tm = _pick_tile(512, n_lat, batch * n_ctx)
    dims = (batch, n_lat, n_ctx, tm)

    h = jnp.concatenate([x.reshape(batch * n_lat, d), ctx.reshape(batch * n_ctx, d)], 0).astype(F32)
    cond = jnp.concatenate([c, c_ctx[None], jnp.zeros((SUBLANES - batch - 1, d), c.dtype)], 0).astype(F32)
    ada = _ada_table(cond, w_ada, b_ada)
    mods = jnp.pad(ada.reshape(depth, SUBLANES, 6, d), ((0, 0), (0, 0), (0, SUBLANES - 6), (0, 0)))
    rope = _rope_tables(n_lat, tm)

    for i in range(depth):
        m, j = i % N_MIXERS, i // N_MIXERS
        mod = mods[i]
        lng1, lnb1 = ln_g[i, 0][None], ln_b[i, 0][None]
        lng2, lnb2 = ln_g[i, 1][None], ln_b[i, 1][None]
        if m == 0:
            h = _attention_layer(h, mod, a_wqkv[j], a_qg[j], a_kg[j], a_wo[j], lng1, lnb1, rope, dims, alpha)
        elif m == 1:
            h = _gmlp_layer(h, mod, b_win[j], b_vg[j], b_vb[j], b_ws[j], b_bs[j], b_wout[j], lng1, lnb1, dims, alpha)
        else:
            h = _deltanet_layer(h, mod, c_win[j], c_conv[j], c_alog[j], c_dtb[j], c_ng[j], c_wout[j],
                                lng1, lnb1, dims, alpha)
        h = _moe_layer(h, mod, w_router, router_bias, e_gate, e_up, e_down, i, lng2, lnb2, dims, alpha)
    return h[:batch * n_lat].reshape(batch, n_lat, d).astype(x.dtype)
```

```python
import functools
import math

import jax
import jax.numpy as jnp
from jax import lax
from jax.experimental import pallas as pl
from jax.experimental.pallas import tpu as pltpu

F32 = jnp.float32
BF16 = jnp.bfloat16

GRID_W = 64
A_HEADS = 8
A_KV_HEADS = 2
A_GROUP = A_HEADS // A_KV_HEADS
HEAD_DIM = 128
ROPE_THETA = 10000.0
B_CHUNK = 128
B_GROUPS = 8
C_HEADS = 8
C_CONV = 3
C_CHUNK = 64
N_EXPERTS = 16
N_GROUPS = 4
TOP_K = 2
EPS = 1e-6
N_MIXERS = 3

LANES = 128
SUBLANES = 8
VMEM_LIMIT_BYTES = 56 * 1024 * 1024
LOG2E = 1.4426950408889634

MOE_ROWS = 512


def _cparams(sem, vmem=VMEM_LIMIT_BYTES):
    return pltpu.CompilerParams(dimension_semantics=sem, vmem_limit_bytes=vmem)


def _pick_tile(cap, *sizes):
    t = cap
    while t > SUBLANES and any(s % t for s in sizes):
        t //= 2
    assert all(s % t == 0 for s in sizes), (cap, sizes)
    return t


def _layer_norm(x, g, b):
    mu = jnp.mean(x, -1, keepdims=True)
    xc = x - mu
    var = jnp.mean(xc * xc, -1, keepdims=True)
    return xc * lax.rsqrt(var + EPS) * g + b


def _silu(x):
    return x * jax.nn.sigmoid(x)


def _bdot(a, b):
    return jnp.dot(a.astype(BF16), b.astype(BF16), preferred_element_type=F32)


def _bdot_nt(a, b):
    return lax.dot_general(a.astype(BF16), b.astype(BF16), (((1,), (1,)), ((), ())), preferred_element_type=F32)


def _split3(x):
    x1 = x.astype(BF16)
    r = x - x1.astype(F32)
    x2 = r.astype(BF16)
    x3 = (r - x2.astype(F32)).astype(BF16)
    return x1, x2, x3


def _ada_kernel(c_ref, w_ref, b_ref, o_ref):
    s = _silu(c_ref[...])
    o_ref[...] = jnp.dot(s, w_ref[...], preferred_element_type=F32, precision=lax.Precision.HIGHEST) + b_ref[...]


def _ada_table(cond, w_ada, b_ada):
    depth, d, d6 = w_ada.shape
    tn = _pick_tile(1536, d6)
    return pl.pallas_call(
        _ada_kernel,
        grid=(depth, d6 // tn),
        in_specs=[
            pl.BlockSpec((SUBLANES, d), lambda i, j: (0, 0)),
            pl.BlockSpec((None, d, tn), lambda i, j: (i, 0, j)),
            pl.BlockSpec((None, 1, tn), lambda i, j: (i, 0, j)),
        ],
        out_specs=pl.BlockSpec((None, SUBLANES, tn), lambda i, j: (i, 0, j)),
        out_shape=jax.ShapeDtypeStruct((depth, SUBLANES, d6), F32),
        compiler_params=_cparams(("arbitrary", "arbitrary")),
        name="ada_table",
    )(cond, w_ada, b_ada.reshape(depth, 1, d6))


SH1, SC1, G1, SH2, SC2, G2 = range(6)


def _mod_spec(d, blocks_per_batch, batch):
    return pl.BlockSpec((None, SUBLANES, d), lambda i, *_: (jnp.minimum(i // blocks_per_batch, batch), 0, 0))


def _modulate(h, m_ref, sh, sc):
    return h * (1.0 + m_ref[sc:sc + 1, :]) + m_ref[sh:sh + 1, :]


def _residual_ln(h, y, m_ref, gate, lng_ref, lnb_ref, alpha):
    return _layer_norm(alpha * h + m_ref[gate:gate + 1, :] * y, lng_ref[...], lnb_ref[...])


def _attn_proj_kernel(h_ref, m_ref, w_ref, qg_ref, kg_ref, cos_ref, sin_ref, q_ref, k_ref, v_ref):
    u = _modulate(h_ref[...], m_ref, SH1, SC1)
    qkv = _bdot(u, w_ref[...])
    cos = cos_ref[...]
    sin = sin_ref[...]
    nq = A_HEADS * HEAD_DIM
    nk = A_KV_HEADS * HEAD_DIM

    def norm_rope(xh, g):
        xn = xh * lax.rsqrt(jnp.mean(xh * xh, -1, keepdims=True) + EPS) * g
        return xn * cos + pltpu.roll(xn, HEAD_DIM // 2, 1) * sin

    for hd in range(A_HEADS):
        sl = slice(hd * HEAD_DIM, (hd + 1) * HEAD_DIM)
        q_ref[:, sl] = norm_rope(qkv[:, sl], qg_ref[...]).astype(BF16)
    for hd in range(A_KV_HEADS):
        sl = slice(hd * HEAD_DIM, (hd + 1) * HEAD_DIM)
        k_ref[:, sl] = norm_rope(qkv[:, nq + hd * HEAD_DIM:nq + (hd + 1) * HEAD_DIM], kg_ref[...]).astype(BF16)
    v_ref[...] = qkv[:, nq + nk:].astype(BF16)


def _attn_proj(h, mod, w_qkv, qg, kg, cos_t, sin_t, tm, n_lat, batch):
    t, d = h.shape
    nq = A_HEADS * HEAD_DIM
    nk = A_KV_HEADS * HEAD_DIM
    bpb = n_lat // tm

    def rope_map(i):
        return (jnp.where(i < batch * bpb, i % bpb, bpb), 0)

    return pl.pallas_call(
        _attn_proj_kernel,
        grid=(t // tm,),
        in_specs=[
            pl.BlockSpec((tm, d), lambda i: (i, 0)),
            _mod_spec(d, bpb, batch),
            pl.BlockSpec((d, nq + 2 * nk), lambda i: (0, 0)),
            pl.BlockSpec((1, HEAD_DIM), lambda i: (0, 0)),
            pl.BlockSpec((1, HEAD_DIM), lambda i: (0, 0)),
            pl.BlockSpec((tm, HEAD_DIM), rope_map),
            pl.BlockSpec((tm, HEAD_DIM), rope_map),
        ],
        out_specs=[
            pl.BlockSpec((tm, nq), lambda i: (i, 0)),
            pl.BlockSpec((tm, nk), lambda i: (i, 0)),
            pl.BlockSpec((tm, nk), lambda i: (i, 0)),
        ],
        out_shape=[
            jax.ShapeDtypeStruct((t, nq), BF16),
            jax.ShapeDtypeStruct((t, nk), BF16),
            jax.ShapeDtypeStruct((t, nk), BF16),
        ],
        compiler_params=_cparams(("parallel",)),
        name="attn_proj",
    )(h, mod, w_qkv, qg, kg, cos_t, sin_t)


def _flash_kernel(q_ref, k_ref, v_ref, o_ref, m_ref, l_ref, acc_ref, *, nchunk, tk):
    m_ref[...] = jnp.full(m_ref.shape, -jnp.inf, F32)
    l_ref[...] = jnp.zeros(l_ref.shape, F32)
    acc_ref[...] = jnp.zeros(acc_ref.shape, F32)
    nslab = tk // LANES

    def body(j, carry):
        off = pl.multiple_of(j * tk, tk)
        kj = k_ref[pl.ds(off, tk), :]
        vj = v_ref[pl.ds(off, tk), :]
        def scores(g):
            qg = q_ref[:, g * HEAD_DIM:(g + 1) * HEAD_DIM]
            return lax.dot_general(qg, kj, (((1,), (1,)), ((), ())), preferred_element_type=F32)

        sc_next = scores(0)
        for g in range(A_GROUP):
            sc = sc_next
            if g + 1 < A_GROUP:
                sc_next = scores(g + 1)
            slabs = [sc[:, c * LANES:(c + 1) * LANES] for c in range(nslab)]
            mc = slabs[0]
            for sl in slabs[1:]:
                mc = jnp.maximum(mc, sl)
            m_old = m_ref[g]
            m_new = jnp.maximum(m_old, jnp.max(mc, -1, keepdims=True))
            alpha = jnp.exp2(m_old - m_new)
            ps = [jnp.exp2(sl - m_new) for sl in slabs]
            lsum = ps[0]
            for x in ps[1:]:
                lsum = lsum + x
            p = jnp.concatenate([x.astype(BF16) for x in ps], axis=1)
            l_ref[g] = alpha * l_ref[g] + lsum
            acc_ref[g] = alpha * acc_ref[g] + jnp.dot(p, vj, preferred_element_type=F32)
            m_ref[g] = m_new
        return carry

    lax.fori_loop(0, nchunk, body, 0)
    for g in range(A_GROUP):
        out = acc_ref[g] / jnp.sum(l_ref[g], -1, keepdims=True)
        o_ref[:, g * HEAD_DIM:(g + 1) * HEAD_DIM] = out.astype(o_ref.dtype)


def _key_tile(n):
    for tk in (768, 512, 384, 256, 128):
        if n % tk == 0:
            return tk
    raise ValueError(n)


def _flash(q, k, v, *, batch, q_row0, q_len, tq):
    gw = A_GROUP * HEAD_DIM
    nqb = q_len // tq
    n_keys = k.shape[1]
    tk = _key_tile(n_keys)
    kv_spec = pl.BlockSpec((None, n_keys, HEAD_DIM), lambda b, kv, i: (b, 0, kv))
    return pl.pallas_call(
        functools.partial(_flash_kernel, nchunk=n_keys // tk, tk=tk),
        grid=(batch, A_KV_HEADS, nqb),
        in_specs=[pl.BlockSpec((tq, gw), lambda b, kv, i: (q_row0 // tq + b * nqb + i, kv)), kv_spec, kv_spec],
        out_specs=pl.BlockSpec((tq, gw), lambda b, kv, i: (b * nqb + i, kv)),
        out_shape=jax.ShapeDtypeStruct((batch * q_len, A_HEADS * HEAD_DIM), BF16),
        scratch_shapes=[pltpu.VMEM((A_GROUP, tq, LANES), F32)] * 2 + [pltpu.VMEM((A_GROUP, tq, HEAD_DIM), F32)],
        compiler_params=_cparams(("parallel", "parallel", "arbitrary")),
        name="flash_attn",
    )(q, k, v)


def _out_proj_kernel(a_ref, w_ref, h_ref, m_ref, lng_ref, lnb_ref, o_ref, *, alpha):
    y = jnp.dot(a_ref[...], w_ref[...], preferred_element_type=F32)
    o_ref[...] = _residual_ln(h_ref[...], y, m_ref, G1, lng_ref, lnb_ref, alpha)


def _out_proj(a, w, h, mod, lng, lnb, tm, n_lat, batch, alpha):
    t, d = h.shape
    ka = a.shape[1]
    return pl.pallas_call(
        functools.partial(_out_proj_kernel, alpha=alpha),
        grid=(t // tm,),
        in_specs=[
            pl.BlockSpec((tm, ka), lambda i: (i, 0)),
            pl.BlockSpec((ka, d), lambda i: (0, 0)),
            pl.BlockSpec((tm, d), lambda i: (i, 0)),
            _mod_spec(d, n_lat // tm, batch),
            pl.BlockSpec((1, d), lambda i: (0, 0)),
            pl.BlockSpec((1, d), lambda i: (0, 0)),
        ],
        out_specs=pl.BlockSpec((tm, d), lambda i: (i, 0)),
        out_shape=jax.ShapeDtypeStruct((t, d), F32),
        compiler_params=_cparams(("parallel",)),
        name="out_proj_ln",
    )(a, w, h, mod, lng, lnb)


def _rope_tables(n, tm):
    rows = n // GRID_W
    half = HEAD_DIM // 2
    inv = 1.0 / (ROPE_THETA ** (jnp.arange(0, half, 2, dtype=F32) / half))
    r = jnp.broadcast_to(jnp.arange(rows, dtype=F32)[:, None], (rows, GRID_W)).reshape(-1)
    col = jnp.broadcast_to(jnp.arange(GRID_W, dtype=F32)[None, :], (rows, GRID_W)).reshape(-1)
    ang = jnp.concatenate([r[:, None] * inv, col[:, None] * inv], -1)
    cos, sin = jnp.cos(ang), jnp.sin(ang)
    cos_t = jnp.concatenate([jnp.concatenate([cos, cos], -1), jnp.ones((tm, HEAD_DIM), F32)], 0)
    sin_t = jnp.concatenate([jnp.concatenate([-sin, sin], -1), jnp.zeros((tm, HEAD_DIM), F32)], 0)
    return cos_t, sin_t


def _deinterleave_heads(w, n_heads):
    lead = w.shape[:-1]
    w = w.reshape(*lead, n_heads, HEAD_DIM // 2, 2)
    return jnp.swapaxes(w, -1, -2).reshape(*lead, n_heads * HEAD_DIM)


def _attention_layer(h, mod, w_qkv, q_g, k_g, w_o, lng, lnb, rope, dims, alpha):
    batch, n_lat, n_ctx, tm = dims
    nq = A_HEADS * HEAD_DIM
    nk = A_KV_HEADS * HEAD_DIM
    w = jnp.concatenate([
        _deinterleave_heads(w_qkv[:, :nq], A_HEADS),
        _deinterleave_heads(w_qkv[:, nq:nq + nk], A_KV_HEADS),
        w_qkv[:, nq + nk:]], -1).astype(BF16)
    qg = _deinterleave_heads(q_g[None, :], 1) * (HEAD_DIM ** -0.5 * LOG2E)
    kg = _deinterleave_heads(k_g[None, :], 1)
    q, k, v = _attn_proj(h, mod, w, qg, kg, rope[0], rope[1], tm, n_lat, batch)
    lat0 = batch * n_lat
    kvw = A_KV_HEADS * HEAD_DIM
    k_ctx, v_ctx = k[lat0:].reshape(batch, n_ctx, kvw), v[lat0:].reshape(batch, n_ctx, kvw)
    k_all = jnp.concatenate([k_ctx, k[:lat0].reshape(batch, n_lat, kvw)], 1)
    v_all = jnp.concatenate([v_ctx, v[:lat0].reshape(batch, n_lat, kvw)], 1)
    o_lat = _flash(q, k_all, v_all, batch=batch, q_row0=0, q_len=n_lat, tq=_pick_tile(256, n_lat))
    o_ctx = _flash(q, k_ctx, v_ctx, batch=batch, q_row0=lat0, q_len=n_ctx, tq=_pick_tile(256, n_ctx))
    o = jnp.concatenate([o_lat, o_ctx], 0)
    return _out_proj(o, w_o.astype(BF16), h, mod, lng, lnb, tm, n_lat, batch, alpha)


def _gmlp_kernel(h_ref, m_ref, win_ref, vg_ref, vb_ref, ws_ref, bs_ref, wout_ref, lng_ref, lnb_ref, o_ref,
                 gated_ref, *, alpha, width):
    h = h_ref[...]
    u = _modulate(h, m_ref, SH1, SC1)
    z = jax.nn.gelu(_bdot(u, win_ref[...]))
    uu = z[:, :width]
    v = _layer_norm(z[:, width:], vg_ref[...], vb_ref[...]).astype(BF16)
    tm = h.shape[0]
    gw = width // B_GROUPS
    for c in range(tm // B_CHUNK):
        rs = slice(c * B_CHUNK, (c + 1) * B_CHUNK)
        for g in range(B_GROUPS):
            cs = slice(g * gw, (g + 1) * gw)
            mixed = jnp.dot(ws_ref[g], v[rs, cs], preferred_element_type=F32) + bs_ref[g]
            gated_ref[rs, cs] = (uu[rs, cs] * mixed).astype(BF16)
    y = jnp.dot(gated_ref[...], wout_ref[...], preferred_element_type=F32)
    o_ref[...] = _residual_ln(h, y, m_ref, G1, lng_ref, lnb_ref, alpha)


def _gmlp_layer(h, mod, w_in, v_g, v_b, w_s, b_s, w_out, lng, lnb, dims, alpha):
    batch, n_lat, n_ctx, tm = dims
    t, d = h.shape
    width = w_out.shape[0]
    gw = width // B_GROUPS
    assert tm % B_CHUNK == 0 and n_ctx % B_CHUNK == 0
    bs = jnp.broadcast_to(b_s[:, :, None], (B_GROUPS, B_CHUNK, gw)).astype(F32)
    const = lambda *shape: pl.BlockSpec(shape, lambda i: (0,) * len(shape))
    return pl.pallas_call(
        functools.partial(_gmlp_kernel, alpha=alpha, width=width),
        grid=(t // tm,),
        in_specs=[
            pl.BlockSpec((tm, d), lambda i: (i, 0)),
            _mod_spec(d, n_lat // tm, batch),
            const(d, 2 * width), const(1, width), const(1, width),
            const(B_GROUPS, B_CHUNK, B_CHUNK), const(B_GROUPS, B_CHUNK, gw),
            const(width, d), const(1, d), const(1, d),
        ],
        out_specs=pl.BlockSpec((tm, d), lambda i: (i, 0)),
        out_shape=jax.ShapeDtypeStruct((t, d), F32),
        scratch_shapes=[pltpu.VMEM((tm, width), BF16)],
        compiler_params=_cparams(("parallel",)),
        name="gmlp_layer",
    )(h, mod, w_in.astype(BF16), v_g[None], v_b[None], w_s.astype(BF16), bs, w_out.astype(BF16), lng, lnb)


def _delta_proj_kernel(h_ref, m_ref, w_ref, o_ref):
    u = _modulate(h_ref[...], m_ref, SH1, SC1)
    o_ref[...] = _bdot(u, w_ref[...])


def _delta_proj(h, mod, w, tm, n_lat, batch):
    t, d = h.shape
    ncol = w.shape[1]
    tn = _pick_tile(1024, ncol - LANES)
    nj = (ncol - LANES) // tn
    return pl.pallas_call(
        _delta_proj_kernel,
        grid=(t // tm, nj),
        in_specs=[
            pl.BlockSpec((tm, d), lambda i, j: (i, 0)),
            pl.BlockSpec((None, SUBLANES, d), lambda i, j: (jnp.minimum(i // (n_lat // tm), batch), 0, 0)),
            pl.BlockSpec((d, tn), lambda i, j: (0, j)),
        ],
        out_specs=pl.BlockSpec((tm, tn), lambda i, j: (i, j)),
        out_shape=jax.ShapeDtypeStruct((t, ncol - LANES), F32),
        compiler_params=_cparams(("parallel", "arbitrary")),
        name="delta_proj",
    )(h, mod, w[:, :ncol - LANES]), pl.pallas_call(
        _delta_proj_kernel,
        grid=(t // tm, 1),
        in_specs=[
            pl.BlockSpec((tm, d), lambda i, j: (i, 0)),
            pl.BlockSpec((None, SUBLANES, d), lambda i, j: (jnp.minimum(i // (n_lat // tm), batch), 0, 0)),
            pl.BlockSpec((d, LANES), lambda i, j: (0, 0)),
        ],
        out_specs=pl.BlockSpec((tm, LANES), lambda i, j: (i, 0)),
        out_shape=jax.ShapeDtypeStruct((t, LANES), F32),
        compiler_params=_cparams(("parallel", "arbitrary")),
        name="delta_proj_gates",
    )(h, mod, w[:, ncol - LANES:])


def _delta_prep_kernel(x_ref, xp_ref, xn_ref, ab_ref, cw_ref, ga_ref, dtb_ref, q_ref, k_ref, v_ref, gb_ref,
                       *, tm, n_lat_rows, n_lat, n_ctx, cw):
    i = pl.program_id(0)
    row0 = i * tm
    in_lat = row0 < n_lat_rows
    pos = jnp.where(in_lat, row0 % n_lat, (row0 - n_lat_rows) % n_ctx)
    seq = jnp.where(in_lat, n_lat, n_ctx)
    has_prev = (pos != 0).astype(F32)
    has_next = (pos + tm != seq).astype(F32)
    rows = lax.broadcasted_iota(jnp.int32, (tm, HEAD_DIM), 0)
    first = rows == 0
    last = rows == tm - 1
    heads = cw // HEAD_DIM
    for part, out_ref in enumerate((q_ref, k_ref, v_ref)):
        for hd in range(heads):
            sl = slice(part * cw + hd * HEAD_DIM, part * cw + (hd + 1) * HEAD_DIM)
            x = x_ref[:, sl]
            prev_row = xp_ref[SUBLANES - 1:SUBLANES, sl] * has_prev
            next_row = xn_ref[0:1, sl] * has_next
            x_prev = jnp.where(first, prev_row, pltpu.roll(x, 1, 0))
            x_next = jnp.where(last, next_row, pltpu.roll(x, tm - 1, 0))
            y = _silu(x_prev * cw_ref[0:1, sl] + x * cw_ref[1:2, sl] + x_next * cw_ref[2:3, sl])
            if part < 2:
                y = y * lax.rsqrt(jnp.sum(y * y, -1, keepdims=True) + EPS)
            if part == 0:
                y = y * (HEAD_DIM ** -0.5)
            out_ref[:, hd * HEAD_DIM:(hd + 1) * HEAD_DIM] = y.astype(out_ref.dtype)
    ab = ab_ref[...]
    xg = ab + dtb_ref[...]
    softplus = jnp.maximum(xg, 0.0) + jnp.log1p(jnp.exp(-jnp.abs(xg)))
    lane = lax.broadcasted_iota(jnp.int32, ab.shape, 1)
    gb_ref[...] = jnp.where(lane < 2 * C_HEADS, ga_ref[...] * softplus, jax.nn.sigmoid(ab))


def _delta_prep(proj, ab, conv_w, a_log, dt_bias, tm, n_lat, n_ctx, batch):
    t = proj.shape[0]
    cw = C_HEADS * HEAD_DIM
    r8 = tm // SUBLANES
    nb8 = t // SUBLANES
    ga = jnp.zeros((1, LANES), F32).at[0, :2 * C_HEADS].set(-jnp.exp(a_log.astype(F32)).reshape(-1))
    dtb = jnp.zeros((1, LANES), F32).at[0, :2 * C_HEADS].set(dt_bias.astype(F32).reshape(-1))
    cwp = jnp.zeros((SUBLANES, 3 * cw), F32).at[:C_CONV].set(conv_w)
    return pl.pallas_call(
        functools.partial(_delta_prep_kernel, tm=tm, n_lat_rows=batch * n_lat, n_lat=n_lat, n_ctx=n_ctx, cw=cw),
        grid=(t // tm,),
        in_specs=[
            pl.BlockSpec((tm, 3 * cw), lambda i: (i, 0)),
            pl.BlockSpec((SUBLANES, 3 * cw), lambda i: (jnp.maximum(i * r8 - 1, 0), 0)),
            pl.BlockSpec((SUBLANES, 3 * cw), lambda i: (jnp.minimum((i + 1) * r8, nb8 - 1), 0)),
            pl.BlockSpec((tm, LANES), lambda i: (i, 0)),
            pl.BlockSpec((SUBLANES, 3 * cw), lambda i: (0, 0)),
            pl.BlockSpec((1, LANES), lambda i: (0, 0)),
            pl.BlockSpec((1, LANES), lambda i: (0, 0)),
        ],
        out_specs=[pl.BlockSpec((tm, cw), lambda i: (i, 0))] * 3 + [pl.BlockSpec((tm, LANES), lambda i: (i, 0))],
        out_shape=[jax.ShapeDtypeStruct((t, cw), BF16)] * 3 + [jax.ShapeDtypeStruct((t, LANES), F32)],
        compiler_params=_cparams(("parallel",)),
        name="delta_prep",
    )(proj, proj, proj, ab, cwp, ga, dtb)


DELTA_BLOCK = 16


def _delta_scan_kernel(qf_ref, kf_ref, vf_ref, gf_ref, qb_ref, kb_ref, vb_ref, gb_ref, of_ref, ob_ref, s_ref):
    c = pl.program_id(1)

    @pl.when(c == 0)
    def _():
        s_ref[...] = jnp.zeros(s_ref.shape, F32)

    L = C_CHUNK
    ri = lax.broadcasted_iota(jnp.int32, (L, L), 0)
    ci = lax.broadcasted_iota(jnp.int32, (L, L), 1)
    eye = (ri == ci).astype(F32)
    blk_mask = (ri // DELTA_BLOCK) == (ci // DELTA_BLOCK)
    dirs = []
    for direction, (q_ref, k_ref, v_ref, g_ref) in enumerate(
            ((qf_ref, kf_ref, vf_ref, gf_ref), (qb_ref, kb_ref, vb_ref, gb_ref))):
        tri, strict = (ri >= ci, ri > ci) if direction == 0 else (ri <= ci, ri < ci)
        tri_b = tri.astype(BF16)
        gb = g_ref[...]
        g1, g2, g3 = _split3(gb)
        gc = (jnp.dot(tri_b, g1, preferred_element_type=F32) + jnp.dot(tri_b, g2, preferred_element_type=F32)
              + jnp.dot(tri_b, g3, preferred_element_type=F32))
        dirs.append((q_ref, k_ref, v_ref, gb, gc, gc.T, tri, strict, L - 1 if direction == 0 else 0))

    chains = [(d, hd) for d in range(2) for hd in range(C_HEADS)]
    nch = len(chains)

    def per_chain(fn):
        return [fn(i, *chains[i]) for i in range(nch)]

    def lanes(hd):
        return slice(hd * HEAD_DIM, (hd + 1) * HEAD_DIM)

    kh = per_chain(lambda i, d, hd: dirs[d][1][:, lanes(hd)])
    qh = per_chain(lambda i, d, hd: dirs[d][0][:, lanes(hd)])
    kq = per_chain(lambda i, d, hd: _bdot_nt(jnp.concatenate([kh[i], qh[i]], 0), kh[i]))
    g_col = per_chain(lambda i, d, hd: dirs[d][4][:, d * C_HEADS + hd:d * C_HEADS + hd + 1])
    g_row = per_chain(lambda i, d, hd: dirs[d][5][d * C_HEADS + hd:d * C_HEADS + hd + 1, :])
    beta = per_chain(lambda i, d, hd: dirs[d][3][:, (2 + d) * C_HEADS + hd:(2 + d) * C_HEADS + hd + 1])
    g_last = per_chain(lambda i, d, hd: g_col[i][dirs[d][8]:dirs[d][8] + 1, :])
    decay = per_chain(lambda i, d, hd: jnp.exp(jnp.where(dirs[d][6], g_col[i] - g_row[i], -jnp.inf)))
    a_mat = per_chain(lambda i, d, hd: jnp.where(dirs[d][7], kq[i][:L] * beta[i] * decay[i], 0.0))
    attn = per_chain(lambda i, d, hd: kq[i][L:] * decay[i])
    eg = per_chain(lambda i, d, hd: jnp.exp(g_col[i]))
    rhs = per_chain(lambda i, d, hd: jnp.concatenate(
        [dirs[d][2][:, lanes(hd)].astype(F32) * beta[i], kh[i].astype(F32) * (beta[i] * eg[i])], 1))
    q_dec = per_chain(lambda i, d, hd: qh[i].astype(F32) * eg[i])
    k_dec = per_chain(lambda i, d, hd: kh[i].astype(F32) * jnp.exp(g_last[i] - g_col[i]))

    n = per_chain(lambda i, d, hd: jnp.where(blk_mask, a_mat[i], 0.0))
    off = per_chain(lambda i, d, hd: a_mat[i] - n[i])
    n2 = per_chain(lambda i, d, hd: _bdot(n[i], n[i]))
    n4 = per_chain(lambda i, d, hd: _bdot(n2[i], n2[i]))
    dv = per_chain(lambda i, d, hd: eye - n[i] + n2[i] - _bdot(n[i], n2[i]))
    n8 = per_chain(lambda i, d, hd: _bdot(n4[i], n4[i]))
    dv = per_chain(lambda i, d, hd: dv[i] + _bdot(dv[i], n4[i]))
    dv = per_chain(lambda i, d, hd: dv[i] + _bdot(dv[i], n8[i]))
    m = per_chain(lambda i, d, hd: _bdot(dv[i], off[i]))
    r1 = per_chain(lambda i, d, hd: _bdot(dv[i], rhs[i]))
    m2 = per_chain(lambda i, d, hd: _bdot(m[i], m[i]))
    x1 = per_chain(lambda i, d, hd: r1[i] - _bdot(m[i], r1[i]))
    uw = per_chain(lambda i, d, hd: x1[i] + _bdot(m2[i], x1[i]))

    s_old = per_chain(lambda i, d, hd: s_ref[i])
    ws = per_chain(lambda i, d, hd: _bdot(jnp.concatenate([uw[i][:, HEAD_DIM:], q_dec[i]], 0), s_old[i]))
    v_new = per_chain(lambda i, d, hd: (uw[i][:, :HEAD_DIM] - ws[i][:L]).astype(BF16))
    o = per_chain(lambda i, d, hd: ws[i][L:] + _bdot(attn[i], v_new[i]))
    kv = per_chain(lambda i, d, hd: lax.dot_general(
        k_dec[i].astype(BF16), v_new[i], (((0,), (0,)), ((), ())), preferred_element_type=F32))
    for i, (d, hd) in enumerate(chains):
        s_ref[i] = s_old[i] * jnp.exp(g_last[i]) + kv[i]
        (of_ref, ob_ref)[d][:, lanes(hd)] = o[i].astype(of_ref.dtype)


def _delta_scan(q, k, v, gb, n_lat, n_ctx, batch):
    t, cw = q.shape
    nlc, ncc = n_lat // C_CHUNK, n_ctx // C_CHUNK
    lat_blocks = batch * nlc

    def fwd_map(b, c):
        return jnp.where(c < ncc, lat_blocks + b * ncc + c, b * nlc + (c - ncc))

    def bwd_map(b, c):
        return jnp.where(c < ncc, lat_blocks + b * ncc + (ncc - 1 - c), b * nlc + (nlc - 1 - (c - ncc)))

    fspec = lambda w: pl.BlockSpec((C_CHUNK, w), lambda b, c: (fwd_map(b, c), 0))
    bspec = lambda w: pl.BlockSpec((C_CHUNK, w), lambda b, c: (bwd_map(b, c), 0))
    return pl.pallas_call(
        _delta_scan_kernel,
        grid=(batch, ncc + nlc),
        in_specs=[fspec(cw), fspec(cw), fspec(cw), fspec(LANES), bspec(cw), bspec(cw), bspec(cw), bspec(LANES)],
        out_specs=[fspec(cw), bspec(cw)],
        out_shape=[jax.ShapeDtypeStruct((t, cw), BF16)] * 2,
        scratch_shapes=[pltpu.VMEM((2 * C_HEADS, HEAD_DIM, HEAD_DIM), F32)],
        compiler_params=_cparams(("parallel", "arbitrary")),
        name="delta_scan",
    )(q, k, v, gb, q, k, v, gb)


def _delta_out_kernel(of_ref, ob_ref, z_ref, ng_ref, w_ref, h_ref, m_ref, lng_ref, lnb_ref, o_ref, y_ref, *, alpha):
    for hd in range(C_HEADS):
        sl = slice(hd * HEAD_DIM, (hd + 1) * HEAD_DIM)
        o = of_ref[:, sl].astype(F32) + ob_ref[:, sl].astype(F32)
        on = o * lax.rsqrt(jnp.mean(o * o, -1, keepdims=True) + EPS) * ng_ref[...]
        y_ref[:, sl] = (on * _silu(z_ref[:, sl])).astype(BF16)
    y = jnp.dot(y_ref[...], w_ref[...], preferred_element_type=F32)
    o_ref[...] = _residual_ln(h_ref[...], y, m_ref, G1, lng_ref, lnb_ref, alpha)


def _delta_out(o_f, o_b, proj, norm_g, w_out, h, mod, lng, lnb, tm, n_lat, batch, alpha):
    t, d = h.shape
    cw = C_HEADS * HEAD_DIM
    row = lambda w: pl.BlockSpec((tm, w), lambda i: (i, 0))
    return pl.pallas_call(
        functools.partial(_delta_out_kernel, alpha=alpha),
        grid=(t // tm,),
        in_specs=[
            row(cw), row(cw),
            pl.BlockSpec((tm, cw), lambda i: (i, 3)),
            pl.BlockSpec((1, HEAD_DIM), lambda i: (0, 0)),
            pl.BlockSpec((cw, d), lambda i: (0, 0)),
            row(d),
            _mod_spec(d, n_lat // tm, batch),
            pl.BlockSpec((1, d), lambda i: (0, 0)),
            pl.BlockSpec((1, d), lambda i: (0, 0)),
        ],
        out_specs=row(d),
        out_shape=jax.ShapeDtypeStruct((t, d), F32),
        scratch_shapes=[pltpu.VMEM((tm, cw), BF16)],
        compiler_params=_cparams(("parallel",)),
        name="delta_out_ln",
    )(o_f, o_b, proj, norm_g[None], w_out, h, mod, lng, lnb)


def _deltanet_layer(h, mod, w_in, conv_w, a_log, dt_bias, norm_g, w_out, lng, lnb, dims, alpha):
    batch, n_lat, n_ctx, tm = dims
    cw = C_HEADS * HEAD_DIM
    d = h.shape[1]
    n_gate = w_in.shape[1] - 4 * cw
    w = jnp.concatenate([w_in, jnp.zeros((d, LANES - n_gate), w_in.dtype)], 1).astype(BF16)
    proj, ab = _delta_proj(h, mod, w, tm, n_lat, batch)
    tmc = _pick_tile(256, n_lat, n_ctx)
    q, k, v, gb = _delta_prep(proj, ab, conv_w, a_log, dt_bias, tmc, n_lat, n_ctx, batch)
    o_f, o_b = _delta_scan(q, k, v, gb, n_lat, n_ctx, batch)
    return _delta_out(o_f, o_b, proj, norm_g, w_out.astype(BF16), h, mod, lng, lnb, tm, n_lat, batch, alpha)


def _router_kernel(h_ref, m_ref, wr_ref, bias_ref, u_ref, idx_ref, wt_ref, cnt_ref, carry_ref):
    i = pl.program_id(0)

    @pl.when(i == 0)
    def _():
        carry_ref[...] = jnp.zeros(carry_ref.shape, F32)

    u = _modulate(h_ref[...], m_ref, SH2, SC2)
    u_ref[...] = u.astype(BF16)
    tm = u.shape[0]
    u1, u2, _ = _split3(u)
    w1, w2, _ = _split3(wr_ref[...])
    nt = lambda a, b: lax.dot_general(a, b, (((1,), (1,)), ((), ())), preferred_element_type=F32)
    logits = nt(w1, u1) + nt(w1, u2) + nt(w2, u1)
    scores = jax.nn.sigmoid(logits)
    sel = scores + bias_ref[:, 0:1]
    per = N_EXPERTS // N_GROUPS
    best = None
    gidx = jnp.zeros((1, tm), jnp.int32)
    for g in range(N_GROUPS):
        r = [sel[g * per + j:g * per + j + 1, :] for j in range(per)]
        top2 = None
        for a in range(per):
            for b in range(a + 1, per):
                pair = r[a] + r[b]
                top2 = pair if top2 is None else jnp.maximum(top2, pair)
        if best is None:
            best = top2
        else:
            better = top2 > best
            gidx = jnp.where(better, g, gidx)
            best = jnp.where(better, top2, best)
    eio = lax.broadcasted_iota(jnp.int32, (N_EXPERTS, tm), 0)
    masked = jnp.where(eio // per == gidx, sel, -jnp.inf)
    m1 = jnp.max(masked, 0, keepdims=True)
    i1 = jnp.min(jnp.where(masked == m1, eio, N_EXPERTS), 0, keepdims=True)
    masked2 = jnp.where(eio == i1, -jnp.inf, masked)
    m2 = jnp.max(masked2, 0, keepdims=True)
    i2 = jnp.min(jnp.where(masked2 == m2, eio, N_EXPERTS), 0, keepdims=True)
    oh1 = eio == i1
    oh2 = eio == i2
    s1 = jnp.sum(jnp.where(oh1, scores, 0.0), 0, keepdims=True)
    s2 = jnp.sum(jnp.where(oh2, scores, 0.0), 0, keepdims=True)
    tot = s1 + s2
    oh = jnp.where(oh1 | oh2, 1.0, 0.0)
    ti = lax.broadcasted_iota(jnp.int32, (tm, tm), 0)
    tj = lax.broadcasted_iota(jnp.int32, (tm, tm), 1)
    before = (ti < tj).astype(BF16)
    rank = carry_ref[:, 0:1] + jnp.dot(oh.astype(BF16), before, preferred_element_type=F32)
    r1 = jnp.sum(jnp.where(oh1, rank, 0.0), 0, keepdims=True)
    r2 = jnp.sum(jnp.where(oh2, rank, 0.0), 0, keepdims=True)
    new_carry = carry_ref[...] + jnp.sum(oh, 1, keepdims=True)
    carry_ref[...] = new_carry
    cnt_ref[...] = new_carry
    zi = jnp.zeros((SUBLANES - 4, tm), jnp.int32)
    idx_ref[...] = jnp.concatenate([i1, i2, r1.astype(jnp.int32), r2.astype(jnp.int32), zi], 0)
    wt_ref[...] = jnp.concatenate([s1 / tot, s2 / tot, jnp.zeros((SUBLANES - 2, tm), F32)], 0)


def _router(h, mod, w_router, router_bias, tm, n_lat, batch):
    t, d = h.shape
    wr = w_router.T.astype(F32)
    bias = jnp.broadcast_to(router_bias.astype(F32)[:, None], (N_EXPERTS, LANES))
    return pl.pallas_call(
        _router_kernel,
        grid=(t // tm,),
        in_specs=[
            pl.BlockSpec((tm, d), lambda i: (i, 0)),
            _mod_spec(d, n_lat // tm, batch),
            pl.BlockSpec((N_EXPERTS, d), lambda i: (0, 0)),
            pl.BlockSpec((N_EXPERTS, LANES), lambda i: (0, 0)),
        ],
        out_specs=[
            pl.BlockSpec((tm, d), lambda i: (i, 0)),
            pl.BlockSpec((SUBLANES, tm), lambda i: (0, i)),
            pl.BlockSpec((SUBLANES, tm), lambda i: (0, i)),
            pl.BlockSpec((N_EXPERTS, LANES), lambda i: (0, 0)),
        ],
        out_shape=[
            jax.ShapeDtypeStruct((t, d), BF16),
            jax.ShapeDtypeStruct((SUBLANES, t), jnp.int32),
            jax.ShapeDtypeStruct((SUBLANES, t), F32),
            jax.ShapeDtypeStruct((N_EXPERTS, LANES), F32),
        ],
        scratch_shapes=[pltpu.VMEM((N_EXPERTS, LANES), F32)],
        compiler_params=_cparams(("arbitrary",)),
        name="moe_router",
    )(h, mod, wr, bias)


def _expert_kernel(be_ref, nu_ref, x_ref, wg_ref, wu_ref, wd_ref, o_ref):
    i = pl.program_id(0)

    @pl.when(i < nu_ref[0])
    def _():
        x = x_ref[...]
        hid = _silu(_bdot(x, wg_ref[...])) * _bdot(x, wu_ref[...])
        o_ref[...] = _bdot(hid, wd_ref[...]).astype(o_ref.dtype)

    @pl.when(i >= nu_ref[0])
    def _():
        o_ref[...] = jnp.zeros(o_ref.shape, o_ref.dtype)


def _experts(xs, block_e, n_used, e_gate, e_up, e_down, layer):
    n_rows, d = xs.shape
    de = e_gate.shape[-1]
    nb = n_rows // MOE_ROWS
    grid_spec = pltpu.PrefetchScalarGridSpec(
        num_scalar_prefetch=2,
        grid=(nb,),
        in_specs=[
            pl.BlockSpec((MOE_ROWS, d), lambda i, be, nu: (i, 0)),
            pl.BlockSpec((None, None, d, de), lambda i, be, nu: (layer, be[i], 0, 0)),
            pl.BlockSpec((None, None, d, de), lambda i, be, nu: (layer, be[i], 0, 0)),
            pl.BlockSpec((None, None, de, d), lambda i, be, nu: (layer, be[i], 0, 0)),
        ],
        out_specs=pl.BlockSpec((MOE_ROWS, d), lambda i, be, nu: (i, 0)),
    )
    return pl.pallas_call(
        _expert_kernel,
        grid_spec=grid_spec,
        out_shape=jax.ShapeDtypeStruct((n_rows, d), BF16),
        compiler_params=_cparams(("arbitrary",)),
        name="moe_experts",
    )(block_e, n_used, xs, e_gate, e_up, e_down)


def _moe_combine_kernel(y_ref, h_ref, m_ref, lng_ref, lnb_ref, o_ref, *, alpha):
    o_ref[...] = _residual_ln(h_ref[...], y_ref[...].astype(F32), m_ref, G2, lng_ref, lnb_ref, alpha)


def _moe_combine(y, h, mod, lng, lnb, tm, n_lat, batch, alpha):
    t, d = h.shape
    row = pl.BlockSpec((tm, d), lambda i: (i, 0))
    return pl.pallas_call(
        functools.partial(_moe_combine_kernel, alpha=alpha),
        grid=(t // tm,),
        in_specs=[row, row, _mod_spec(d, n_lat // tm, batch),
                  pl.BlockSpec((1, d), lambda i: (0, 0)), pl.BlockSpec((1, d), lambda i: (0, 0))],
        out_specs=row,
        out_shape=jax.ShapeDtypeStruct((t, d), F32),
        compiler_params=_cparams(("parallel",)),
        name="moe_combine_ln",
    )(y, h, mod, lng, lnb)


def _moe_layer(h, mod, w_router, router_bias, e_gate, e_up, e_down, layer, lng, lnb, dims, alpha):
    batch, n_lat, n_ctx, tm = dims
    t, d = h.shape
    tr = _pick_tile(256, n_lat, batch * n_ctx)
    u, idx, wts, cnt = _router(h, mod, w_router, router_bias, tr, n_lat, batch)
    counts = cnt[:, 0].astype(jnp.int32)
    padded = (counts + MOE_ROWS - 1) // MOE_ROWS * MOE_ROWS
    pend = jnp.cumsum(padded)
    pstart = pend - padded
    nb = -(-(t * TOP_K + N_EXPERTS * (MOE_ROWS - 1)) // MOE_ROWS)
    n_rows = nb * MOE_ROWS
    eidx, rank = idx[0:TOP_K], idx[TOP_K:2 * TOP_K]
    dest = pstart[eidx] + rank
    tok = jnp.broadcast_to(jnp.arange(t, dtype=jnp.int32)[None, :], (TOP_K, t))
    row_tok = jnp.zeros((n_rows,), jnp.int32).at[dest.reshape(-1)].set(tok.reshape(-1), unique_indices=True)
    starts = jnp.arange(nb, dtype=jnp.int32) * MOE_ROWS
    block_e = jnp.minimum(jnp.sum((pend[None, :] <= starts[:, None]).astype(jnp.int32), 1), N_EXPERTS - 1)
    n_used = (pend[-1:] // MOE_ROWS).astype(jnp.int32)
    xs = jnp.take(u, row_tok, axis=0)
    yb = _experts(xs, block_e, n_used, e_gate, e_up, e_down, layer)
    w = wts[0:TOP_K]
    y = (jnp.take(yb, dest[0], axis=0).astype(F32) * w[0][:, None]
         + jnp.take(yb, dest[1], axis=0).astype(F32) * w[1][:, None])
    return _moe_combine(y, h, mod, lng, lnb, tm, n_lat, batch, alpha)


def kernel(x, c, ctx, c_ctx, w_ada, b_ada, ln_g, ln_b, a_wqkv, a_qg, a_kg, a_wo, b_win, b_vg, b_vb, b_ws, b_bs,
           b_wout, c_win, c_conv, c_alog, c_dtb, c_ng, c_wout, w_router, router_bias, e_gate, e_up, e_down):
    batch, n_lat, d = x.shape
    n_ctx = ctx.shape[1]
    depth = w_ada.shape[0]
    alpha = (2 * depth) ** 0.25
    assert batch + 1 <= SUBLANES
    tm = _pick_tile(512, n_lat, batch * n_ctx)
    dims = (batch, n_lat, n_ctx, tm)

    h = jnp.concatenate([x.reshape(batch * n_lat, d), ctx.reshape(batch * n_ctx, d)], 0).astype(F32)
    cond = jnp.concatenate([c, c_ctx[None], jnp.zeros((SUBLANES - batch - 1, d), c.dtype)], 0).astype(F32)
    ada = _ada_table(cond, w_ada, b_ada)
    mods = jnp.pad(ada.reshape(depth, SUBLANES, 6, d), ((0, 0), (0, 0), (0, SUBLANES - 6), (0, 0)))
    rope = _rope_tables(n_lat, tm)

    for i in range(depth):
        m, j = i % N_MIXERS, i // N_MIXERS
        mod = mods[i]
        lng1, lnb1 = ln_g[i, 0][None], ln_b[i, 0][None]
        lng2, lnb2 = ln_g[i, 1][None], ln_b[i, 1][None]
        if m == 0:
            h = _attention_layer(h, mod, a_wqkv[j], a_qg[j], a_kg[j], a_wo[j], lng1, lnb1, rope, dims, alpha)
        elif m == 1:
            h = _gmlp_layer(h, mod, b_win[j], b_vg[j], b_vb[j], b_ws[j], b_bs[j], b_wout[j], lng1, lnb1, dims, alpha)
        else:
            h = _deltanet_layer(h, mod, c_win[j], c_conv[j], c_alog[j], c_dtb[j], c_ng[j], c_wout[j],
                                lng1, lnb1, dims, alpha)
        h = _moe_layer(h, mod, w_router, router_bias, e_gate, e_up, e_down, i, lng2, lnb2, dims, alpha)
    return h[:batch * n_lat].reshape(batch, n_lat, d).astype(x.dtype)
```

```python
import functools
import math

import jax
import jax.numpy as jnp
from jax import lax
from jax.experimental import pallas as pl
from jax.experimental.pallas import tpu as pltpu

F32 = jnp.float32
BF16 = jnp.bfloat16

GRID_W = 64
A_HEADS = 8
A_KV_HEADS = 2
A_GROUP = A_HEADS // A_KV_HEADS
HEAD_DIM = 128
ROPE_THETA = 10000.0
B_CHUNK = 128
B_GROUPS = 8
C_HEADS = 8
C_CONV = 3
C_CHUNK = 64
N_EXPERTS = 16
N_GROUPS = 4
TOP_K = 2
EPS = 1e-6
N_MIXERS = 3

LANES = 128
SUBLANES = 8
VMEM_LIMIT_BYTES = 56 * 1024 * 1024
LOG2E = 1.4426950408889634

MOE_ROWS = 512


def _cparams(sem, vmem=VMEM_LIMIT_BYTES):
    return pltpu.CompilerParams(dimension_semantics=sem, vmem_limit_bytes=vmem)


def _pick_tile(cap, *sizes):
    t = cap
    while t > SUBLANES and any(s % t for s in sizes):
        t //= 2
    assert all(s % t == 0 for s in sizes), (cap, sizes)
    return t


def _layer_norm(x, g, b):
    mu = jnp.mean(x, -1, keepdims=True)
    xc = x - mu
    var = jnp.mean(xc * xc, -1, keepdims=True)
    return xc * lax.rsqrt(var + EPS) * g + b


def _silu(x):
    return x * jax.nn.sigmoid(x)


def _bdot(a, b):
    return jnp.dot(a.astype(BF16), b.astype(BF16), preferred_element_type=F32)


def _bdot_nt(a, b):
    return lax.dot_general(a.astype(BF16), b.astype(BF16), (((1,), (1,)), ((), ())), preferred_element_type=F32)


def _split3(x):
    x1 = x.astype(BF16)
    r = x - x1.astype(F32)
    x2 = r.astype(BF16)
    x3 = (r - x2.astype(F32)).astype(BF16)
    return x1, x2, x3


def _ada_kernel(c_ref, w_ref, b_ref, o_ref):
    s = _silu(c_ref[...])
    o_ref[...] = jnp.dot(s, w_ref[...], preferred_element_type=F32, precision=lax.Precision.HIGHEST) + b_ref[...]


def _ada_table(cond, w_ada, b_ada):
    depth, d, d6 = w_ada.shape
    tn = _pick_tile(1536, d6)
    return pl.pallas_call(
        _ada_kernel,
        grid=(depth, d6 // tn),
        in_specs=[
            pl.BlockSpec((SUBLANES, d), lambda i, j: (0, 0)),
            pl.BlockSpec((None, d, tn), lambda i, j: (i, 0, j)),
            pl.BlockSpec((None, 1, tn), lambda i, j: (i, 0, j)),
        ],
        out_specs=pl.BlockSpec((None, SUBLANES, tn), lambda i, j: (i, 0, j)),
        out_shape=jax.ShapeDtypeStruct((depth, SUBLANES, d6), F32),
        compiler_params=_cparams(("arbitrary", "arbitrary")),
        name="ada_table",
    )(cond, w_ada, b_ada.reshape(depth, 1, d6))


SH1, SC1, G1, SH2, SC2, G2 = range(6)


def _mod_spec(d, blocks_per_batch, batch):
    return pl.BlockSpec((None, SUBLANES, d), lambda i, *_: (jnp.minimum(i // blocks_per_batch, batch), 0, 0))


def _modulate(h, m_ref, sh, sc):
    return h * (1.0 + m_ref[sc:sc + 1, :]) + m_ref[sh:sh + 1, :]


def _residual_ln(h, y, m_ref, gate, lng_ref, lnb_ref, alpha):
    return _layer_norm(alpha * h + m_ref[gate:gate + 1, :] * y, lng_ref[...], lnb_ref[...])


def _attn_proj_kernel(h_ref, m_ref, w_ref, qg_ref, kg_ref, cos_ref, sin_ref, q_ref, k_ref, v_ref):
    u = _modulate(h_ref[...], m_ref, SH1, SC1)
    qkv = _bdot(u, w_ref[...])
    cos = cos_ref[...]
    sin = sin_ref[...]
    nq = A_HEADS * HEAD_DIM
    nk = A_KV_HEADS * HEAD_DIM

    def norm_rope(xh, g):
        xn = xh * lax.rsqrt(jnp.mean(xh * xh, -1, keepdims=True) + EPS) * g
        return xn * cos + pltpu.roll(xn, HEAD_DIM // 2, 1) * sin

    for hd in range(A_HEADS):
        sl = slice(hd * HEAD_DIM, (hd + 1) * HEAD_DIM)
        q_ref[:, sl] = norm_rope(qkv[:, sl], qg_ref[...]).astype(BF16)
    for hd in range(A_KV_HEADS):
        sl = slice(hd * HEAD_DIM, (hd + 1) * HEAD_DIM)
        k_ref[:, sl] = norm_rope(qkv[:, nq + hd * HEAD_DIM:nq + (hd + 1) * HEAD_DIM], kg_ref[...]).astype(BF16)
    v_ref[...] = qkv[:, nq + nk:].astype(BF16)


def _attn_proj(h, mod, w_qkv, qg, kg, cos_t, sin_t, tm, n_lat, batch):
    t, d = h.shape
    nq = A_HEADS * HEAD_DIM
    nk = A_KV_HEADS * HEAD_DIM
    bpb = n_lat // tm

    def rope_map(i):
        return (jnp.where(i < batch * bpb, i % bpb, bpb), 0)

    return pl.pallas_call(
        _attn_proj_kernel,
        grid=(t // tm,),
        in_specs=[
            pl.BlockSpec((tm, d), lambda i: (i, 0)),
            _mod_spec(d, bpb, batch),
            pl.BlockSpec((d, nq + 2 * nk), lambda i: (0, 0)),
            pl.BlockSpec((1, HEAD_DIM), lambda i: (0, 0)),
            pl.BlockSpec((1, HEAD_DIM), lambda i: (0, 0)),
            pl.BlockSpec((tm, HEAD_DIM), rope_map),
            pl.BlockSpec((tm, HEAD_DIM), rope_map),
        ],
        out_specs=[
            pl.BlockSpec((tm, nq), lambda i: (i, 0)),
            pl.BlockSpec((tm, nk), lambda i: (i, 0)),
            pl.BlockSpec((tm, nk), lambda i: (i, 0)),
        ],
        out_shape=[
            jax.ShapeDtypeStruct((t, nq), BF16),
            jax.ShapeDtypeStruct((t, nk), BF16),
            jax.ShapeDtypeStruct((t, nk), BF16),
        ],
        compiler_params=_cparams(("parallel",)),
        name="attn_proj",
    )(h, mod, w_qkv, qg, kg, cos_t, sin_t)


def _flash_kernel(q_ref, k_ref, v_ref, o_ref, m_ref, l_ref, acc_ref, *, nchunk, tk):
    m_ref[...] = jnp.full(m_ref.shape, -jnp.inf, F32)
    l_ref[...] = jnp.zeros(l_ref.shape, F32)
    acc_ref[...] = jnp.zeros(acc_ref.shape, F32)
    nslab = tk // LANES

    def body(j, carry):
        off = pl.multiple_of(j * tk, tk)
        kj = k_ref[pl.ds(off, tk), :]
        vj = v_ref[pl.ds(off, tk), :]

        def scores(g):
            qg = q_ref[:, g * HEAD_DIM:(g + 1) * HEAD_DIM]
            return lax.dot_general(qg, kj, (((1,), (1,)), ((), ())), preferred_element_type=F32)

        sc_next = scores(0)
        for g in range(A_GROUP):
            sc = sc_next
            if g + 1 < A_GROUP:
                sc_next = scores(g + 1)
            slabs = [sc[:, c * LANES:(c + 1) * LANES] for c in range(nslab)]
            mc = slabs[0]
            for sl in slabs[1:]:
                mc = jnp.maximum(mc, sl)
            m_old = m_ref[g]
            m_new = jnp.maximum(m_old, jnp.max(mc, -1, keepdims=True))
            alpha = jnp.exp2(m_old - m_new)
            ps = [jnp.exp2(sl - m_new) for sl in slabs]
            lsum = ps[0]
            for x in ps[1:]:
                lsum = lsum + x
            p = jnp.concatenate([x.astype(BF16) for x in ps], axis=1)
            l_ref[g] = alpha * l_ref[g] + lsum
            acc_ref[g] = alpha * acc_ref[g] + jnp.dot(p, vj, preferred_element_type=F32)
            m_ref[g] = m_new
        return carry

    lax.fori_loop(0, nchunk, body, 0)
    for g in range(A_GROUP):
        out = acc_ref[g] / jnp.sum(l_ref[g], -1, keepdims=True)
        o_ref[:, g * HEAD_DIM:(g + 1) * HEAD_DIM] = out.astype(o_ref.dtype)


def _key_tile(n):
    for tk in (1408, 768, 512, 384, 256, 128):
        if n % tk == 0:
            return tk
    raise ValueError(n)


def _flash(q, k, v, *, batch, q_row0, q_len, tq):
    gw = A_GROUP * HEAD_DIM
    nqb = q_len // tq
    n_keys = k.shape[1]
    tk = _key_tile(n_keys)
    kv_spec = pl.BlockSpec((None, n_keys, HEAD_DIM), lambda b, kv, i: (b, 0, kv))
    return pl.pallas_call(
        functools.partial(_flash_kernel, nchunk=n_keys // tk, tk=tk),
        grid=(batch, A_KV_HEADS, nqb),
        in_specs=[pl.BlockSpec((tq, gw), lambda b, kv, i: (q_row0 // tq + b * nqb + i, kv)), kv_spec, kv_spec],
        out_specs=pl.BlockSpec((tq, gw), lambda b, kv, i: (b * nqb + i, kv)),
        out_shape=jax.ShapeDtypeStruct((batch * q_len, A_HEADS * HEAD_DIM), BF16),
        scratch_shapes=[pltpu.VMEM((A_GROUP, tq, LANES), F32)] * 2 + [pltpu.VMEM((A_GROUP, tq, HEAD_DIM), F32)],
        compiler_params=_cparams(("parallel", "parallel", "arbitrary")),
        name="flash_attn",
    )(q, k, v)


def _out_proj_kernel(a_ref, w_ref, h_ref, m_ref, lng_ref, lnb_ref, o_ref, *, alpha):
    y = jnp.dot(a_ref[...], w_ref[...], preferred_element_type=F32)
    o_ref[...] = _residual_ln(h_ref[...], y, m_ref, G1, lng_ref, lnb_ref, alpha)


def _out_proj(a, w, h, mod, lng, lnb, tm, n_lat, batch, alpha):
    t, d = h.shape
    ka = a.shape[1]
    return pl.pallas_call(
        functools.partial(_out_proj_kernel, alpha=alpha),
        grid=(t // tm,),
        in_specs=[
            pl.BlockSpec((tm, ka), lambda i: (i, 0)),
            pl.BlockSpec((ka, d), lambda i: (0, 0)),
            pl.BlockSpec((tm, d), lambda i: (i, 0)),
            _mod_spec(d, n_lat // tm, batch),
            pl.BlockSpec((1, d), lambda i: (0, 0)),
            pl.BlockSpec((1, d), lambda i: (0, 0)),
        ],
        out_specs=pl.BlockSpec((tm, d), lambda i: (i, 0)),
        out_shape=jax.ShapeDtypeStruct((t, d), F32),
        compiler_params=_cparams(("parallel",)),
        name="out_proj_ln",
    )(a, w, h, mod, lng, lnb)


def _rope_tables(n, tm):
    rows = n // GRID_W
    half = HEAD_DIM // 2
    inv = 1.0 / (ROPE_THETA ** (jnp.arange(0, half, 2, dtype=F32) / half))
    r = jnp.broadcast_to(jnp.arange(rows, dtype=F32)[:, None], (rows, GRID_W)).reshape(-1)
    col = jnp.broadcast_to(jnp.arange(GRID_W, dtype=F32)[None, :], (rows, GRID_W)).reshape(-1)
    ang = jnp.concatenate([r[:, None] * inv, col[:, None] * inv], -1)
    cos, sin = jnp.cos(ang), jnp.sin(ang)
    cos_t = jnp.concatenate([jnp.concatenate([cos, cos], -1), jnp.ones((tm, HEAD_DIM), F32)], 0)
    sin_t = jnp.concatenate([jnp.concatenate([-sin, sin], -1), jnp.zeros((tm, HEAD_DIM), F32)], 0)
    return cos_t, sin_t


def _deinterleave_heads(w, n_heads):
    lead = w.shape[:-1]
    w = w.reshape(*lead, n_heads, HEAD_DIM // 2, 2)
    return jnp.swapaxes(w, -1, -2).reshape(*lead, n_heads * HEAD_DIM)


def _attention_layer(h, mod, w_qkv, q_g, k_g, w_o, lng, lnb, rope, dims, alpha):
    batch, n_lat, n_ctx, tm = dims
    nq = A_HEADS * HEAD_DIM
    nk = A_KV_HEADS * HEAD_DIM
    w = jnp.concatenate([
        _deinterleave_heads(w_qkv[:, :nq], A_HEADS),
        _deinterleave_heads(w_qkv[:, nq:nq + nk], A_KV_HEADS),
        w_qkv[:, nq + nk:]], -1).astype(BF16)
    qg = _deinterleave_heads(q_g[None, :], 1) * (HEAD_DIM ** -0.5 * LOG2E)
    kg = _deinterleave_heads(k_g[None, :], 1)
    q, k, v = _attn_proj(h, mod, w, qg, kg, rope[0], rope[1], tm, n_lat, batch)
    lat0 = batch * n_lat
    kvw = A_KV_HEADS * HEAD_DIM
    k_ctx, v_ctx = k[lat0:].reshape(batch, n_ctx, kvw), v[lat0:].reshape(batch, n_ctx, kvw)
    k_all = jnp.concatenate([k_ctx, k[:lat0].reshape(batch, n_lat, kvw)], 1)
    v_all = jnp.concatenate([v_ctx, v[:lat0].reshape(batch, n_lat, kvw)], 1)
    o_lat = _flash(q, k_all, v_all, batch=batch, q_row0=0, q_len=n_lat, tq=_pick_tile(512, n_lat))
    o_ctx = _flash(q, k_ctx, v_ctx, batch=batch, q_row0=lat0, q_len=n_ctx, tq=_pick_tile(256, n_ctx))
    o = jnp.concatenate([o_lat, o_ctx], 0)
    return _out_proj(o, w_o.astype(BF16), h, mod, lng, lnb, tm, n_lat, batch, alpha)


def _gmlp_kernel(h_ref, m_ref, win_ref, vg_ref, vb_ref, ws_ref, bs_ref, wout_ref, lng_ref, lnb_ref, o_ref,
                 gated_ref, *, alpha, width):
    h = h_ref[...]
    u = _modulate(h, m_ref, SH1, SC1)
    z = jax.nn.gelu(_bdot(u, win_ref[...]))
    uu = z[:, :width]
    v = _layer_norm(z[:, width:], vg_ref[...], vb_ref[...]).astype(BF16)
    tm = h.shape[0]
    gw = width // B_GROUPS
    for c in range(tm // B_CHUNK):
        rs = slice(c * B_CHUNK, (c + 1) * B_CHUNK)
        for g in range(B_GROUPS):
            cs = slice(g * gw, (g + 1) * gw)
            mixed = jnp.dot(ws_ref[g], v[rs, cs], preferred_element_type=F32) + bs_ref[g]
            gated_ref[rs, cs] = (uu[rs, cs] * mixed).astype(BF16)
    y = jnp.dot(gated_ref[...], wout_ref[...], preferred_element_type=F32)
    o_ref[...] = _residual_ln(h, y, m_ref, G1, lng_ref, lnb_ref, alpha)


def _gmlp_layer(h, mod, w_in, v_g, v_b, w_s, b_s, w_out, lng, lnb, dims, alpha):
    batch, n_lat, n_ctx, tm = dims
    t, d = h.shape
    width = w_out.shape[0]
    gw = width // B_GROUPS
    assert tm % B_CHUNK == 0 and n_ctx % B_CHUNK == 0
    bs = jnp.broadcast_to(b_s[:, :, None], (B_GROUPS, B_CHUNK, gw)).astype(F32)
    const = lambda *shape: pl.BlockSpec(shape, lambda i: (0,) * len(shape))
    return pl.pallas_call(
        functools.partial(_gmlp_kernel, alpha=alpha, width=width),
        grid=(t // tm,),
        in_specs=[
            pl.BlockSpec((tm, d), lambda i: (i, 0)),
            _mod_spec(d, n_lat // tm, batch),
            const(d, 2 * width), const(1, width), const(1, width),
            const(B_GROUPS, B_CHUNK, B_CHUNK), const(B_GROUPS, B_CHUNK, gw),
            const(width, d), const(1, d), const(1, d),
        ],
        out_specs=pl.BlockSpec((tm, d), lambda i: (i, 0)),
        out_shape=jax.ShapeDtypeStruct((t, d), F32),
        scratch_shapes=[pltpu.VMEM((tm, width), BF16)],
        compiler_params=_cparams(("parallel",)),
        name="gmlp_layer",
    )(h, mod, w_in.astype(BF16), v_g[None], v_b[None], w_s.astype(BF16), bs, w_out.astype(BF16), lng, lnb)


def _delta_proj_kernel(h_ref, m_ref, w_ref, o_ref):
    u = _modulate(h_ref[...], m_ref, SH1, SC1)
    o_ref[...] = _bdot(u, w_ref[...])


def _delta_proj(h, mod, w, tm, n_lat, batch):
    t, d = h.shape
    ncol = w.shape[1]
    tn = _pick_tile(1024, ncol - LANES)
    nj = (ncol - LANES) // tn
    return pl.pallas_call(
        _delta_proj_kernel,
        grid=(t // tm, nj),
        in_specs=[
            pl.BlockSpec((tm, d), lambda i, j: (i, 0)),
            pl.BlockSpec((None, SUBLANES, d), lambda i, j: (jnp.minimum(i // (n_lat // tm), batch), 0, 0)),
            pl.BlockSpec((d, tn), lambda i, j: (0, j)),
        ],
        out_specs=pl.BlockSpec((tm, tn), lambda i, j: (i, j)),
        out_shape=jax.ShapeDtypeStruct((t, ncol - LANES), F32),
        compiler_params=_cparams(("parallel", "arbitrary")),
        name="delta_proj",
    )(h, mod, w[:, :ncol - LANES]), pl.pallas_call(
        _delta_proj_kernel,
        grid=(t // tm, 1),
        in_specs=[
            pl.BlockSpec((tm, d), lambda i, j: (i, 0)),
            pl.BlockSpec((None, SUBLANES, d), lambda i, j: (jnp.minimum(i // (n_lat // tm), batch), 0, 0)),
            pl.BlockSpec((d, LANES), lambda i, j: (0, 0)),
        ],
        out_specs=pl.BlockSpec((tm, LANES), lambda i, j: (i, 0)),
        out_shape=jax.ShapeDtypeStruct((t, LANES), F32),
        compiler_params=_cparams(("parallel", "arbitrary")),
        name="delta_proj_gates",
    )(h, mod, w[:, ncol - LANES:])


def _delta_prep_kernel(x_ref, xp_ref, xn_ref, ab_ref, cw_ref, ga_ref, dtb_ref, q_ref, k_ref, v_ref, gb_ref,
                       *, tm, n_lat_rows, n_lat, n_ctx, cw):
    i = pl.program_id(0)
    row0 = i * tm
    in_lat = row0 < n_lat_rows
    pos = jnp.where(in_lat, row0 % n_lat, (row0 - n_lat_rows) % n_ctx)
    seq = jnp.where(in_lat, n_lat, n_ctx)
    has_prev = (pos != 0).astype(F32)
    has_next = (pos + tm != seq).astype(F32)
    rows = lax.broadcasted_iota(jnp.int32, (tm, HEAD_DIM), 0)
    first = rows == 0
    last = rows == tm - 1
    heads = cw // HEAD_DIM
    for part, out_ref in enumerate((q_ref, k_ref, v_ref)):
        for hd in range(heads):
            sl = slice(part * cw + hd * HEAD_DIM, part * cw + (hd + 1) * HEAD_DIM)
            x = x_ref[:, sl]
            prev_row = xp_ref[SUBLANES - 1:SUBLANES, sl] * has_prev
            next_row = xn_ref[0:1, sl] * has_next
            x_prev = jnp.where(first, prev_row, pltpu.roll(x, 1, 0))
            x_next = jnp.where(last, next_row, pltpu.roll(x, tm - 1, 0))
            y = _silu(x_prev * cw_ref[0:1, sl] + x * cw_ref[1:2, sl] + x_next * cw_ref[2:3, sl])
            if part < 2:
                y = y * lax.rsqrt(jnp.sum(y * y, -1, keepdims=True) + EPS)
            if part == 0:
                y = y * (HEAD_DIM ** -0.5)
            out_ref[:, hd * HEAD_DIM:(hd + 1) * HEAD_DIM] = y.astype(out_ref.dtype)
    ab = ab_ref[...]
    xg = ab + dtb_ref[...]
    softplus = jnp.maximum(xg, 0.0) + jnp.log1p(jnp.exp(-jnp.abs(xg)))
    lane = lax.broadcasted_iota(jnp.int32, ab.shape, 1)
    gb_ref[...] = jnp.where(lane < 2 * C_HEADS, ga_ref[...] * softplus, jax.nn.sigmoid(ab))


def _delta_prep(proj, ab, conv_w, a_log, dt_bias, tm, n_lat, n_ctx, batch):
    t = proj.shape[0]
    cw = C_HEADS * HEAD_DIM
    r8 = tm // SUBLANES
    nb8 = t // SUBLANES
    ga = jnp.zeros((1, LANES), F32).at[0, :2 * C_HEADS].set(-jnp.exp(a_log.astype(F32)).reshape(-1))
    dtb = jnp.zeros((1, LANES), F32).at[0, :2 * C_HEADS].set(dt_bias.astype(F32).reshape(-1))
    cwp = jnp.zeros((SUBLANES, 3 * cw), F32).at[:C_CONV].set(conv_w)
    return pl.pallas_call(
        functools.partial(_delta_prep_kernel, tm=tm, n_lat_rows=batch * n_lat, n_lat=n_lat, n_ctx=n_ctx, cw=cw),
        grid=(t // tm,),
        in_specs=[
            pl.BlockSpec((tm, 3 * cw), lambda i: (i, 0)),
            pl.BlockSpec((SUBLANES, 3 * cw), lambda i: (jnp.maximum(i * r8 - 1, 0), 0)),
            pl.BlockSpec((SUBLANES, 3 * cw), lambda i: (jnp.minimum((i + 1) * r8, nb8 - 1), 0)),
            pl.BlockSpec((tm, LANES), lambda i: (i, 0)),
            pl.BlockSpec((SUBLANES, 3 * cw), lambda i: (0, 0)),
            pl.BlockSpec((1, LANES), lambda i: (0, 0)),
            pl.BlockSpec((1, LANES), lambda i: (0, 0)),
        ],
        out_specs=[pl.BlockSpec((tm, cw), lambda i: (i, 0))] * 3 + [pl.BlockSpec((tm, LANES), lambda i: (i, 0))],
        out_shape=[jax.ShapeDtypeStruct((t, cw), BF16)] * 3 + [jax.ShapeDtypeStruct((t, LANES), F32)],
        compiler_params=_cparams(("parallel",)),
        name="delta_prep",
    )(proj, proj, proj, ab, cwp, ga, dtb)


DELTA_BLOCK = 16


def _delta_scan_kernel(qf_ref, kf_ref, vf_ref, gf_ref, qb_ref, kb_ref, vb_ref, gb_ref, of_ref, ob_ref, s_ref):
    c = pl.program_id(1)

    @pl.when(c == 0)
    def _():
        s_ref[...] = jnp.zeros(s_ref.shape, F32)

    L = C_CHUNK
    ri = lax.broadcasted_iota(jnp.int32, (L, L), 0)
    ci = lax.broadcasted_iota(jnp.int32, (L, L), 1)
    eye = (ri == ci).astype(F32)
    blk_mask = (ri // DELTA_BLOCK) == (ci // DELTA_BLOCK)
    dirs = []
    for direction, (q_ref, k_ref, v_ref, g_ref) in enumerate(
            ((qf_ref, kf_ref, vf_ref, gf_ref), (qb_ref, kb_ref, vb_ref, gb_ref))):
        tri, strict = (ri >= ci, ri > ci) if direction == 0 else (ri <= ci, ri < ci)
        tri_b = tri.astype(BF16)
        gb = g_ref[...]
        g1, g2, g3 = _split3(gb)
        gc = (jnp.dot(tri_b, g1, preferred_element_type=F32) + jnp.dot(tri_b, g2, preferred_element_type=F32)
              + jnp.dot(tri_b, g3, preferred_element_type=F32))
        dirs.append((q_ref, k_ref, v_ref, gb, gc, gc.T, tri, strict, L - 1 if direction == 0 else 0))

    chains = [(d, hd) for d in range(2) for hd in range(C_HEADS)]
    nch = len(chains)

    def per_chain(fn):
        return [fn(i, *chains[i]) for i in range(nch)]

    def lanes(hd):
        return slice(hd * HEAD_DIM, (hd + 1) * HEAD_DIM)

    kh = per_chain(lambda i, d, hd: dirs[d][1][:, lanes(hd)])
    qh = per_chain(lambda i, d, hd: dirs[d][0][:, lanes(hd)])
    kq = per_chain(lambda i, d, hd: _bdot_nt(jnp.concatenate([kh[i], qh[i]], 0), kh[i]))
    g_col = per_chain(lambda i, d, hd: dirs[d][4][:, d * C_HEADS + hd:d * C_HEADS + hd + 1])
    g_row = per_chain(lambda i, d, hd: dirs[d][5][d * C_HEADS + hd:d * C_HEADS + hd + 1, :])
    beta = per_chain(lambda i, d, hd: dirs[d][3][:, (2 + d) * C_HEADS + hd:(2 + d) * C_HEADS + hd + 1])
    g_last = per_chain(lambda i, d, hd: g_col[i][dirs[d][8]:dirs[d][8] + 1, :])
    decay = per_chain(lambda i, d, hd: jnp.exp(jnp.where(dirs[d][6], g_col[i] - g_row[i], -jnp.inf)))
    a_mat = per_chain(lambda i, d, hd: jnp.where(dirs[d][7], kq[i][:L] * beta[i] * decay[i], 0.0))
    attn = per_chain(lambda i, d, hd: kq[i][L:] * decay[i])
    eg = per_chain(lambda i, d, hd: jnp.exp(g_col[i]))
    rhs = per_chain(lambda i, d, hd: jnp.concatenate(
        [dirs[d][2][:, lanes(hd)].astype(F32) * beta[i], kh[i].astype(F32) * (beta[i] * eg[i])], 1))
    q_dec = per_chain(lambda i, d, hd: qh[i].astype(F32) * eg[i])
    k_dec = per_chain(lambda i, d, hd: kh[i].astype(F32) * jnp.exp(g_last[i] - g_col[i]))

    n = per_chain(lambda i, d, hd: jnp.where(blk_mask, a_mat[i], 0.0))
    off = per_chain(lambda i, d, hd: a_mat[i] - n[i])
    n2 = per_chain(lambda i, d, hd: _bdot(n[i], n[i]))
    n4 = per_chain(lambda i, d, hd: _bdot(n2[i], n2[i]))
    dv = per_chain(lambda i, d, hd: eye - n[i] + n2[i] - _bdot(n[i], n2[i]))
    n8 = per_chain(lambda i, d, hd: _bdot(n4[i], n4[i]))
    dv = per_chain(lambda i, d, hd: dv[i] + _bdot(dv[i], n4[i]))
    dv = per_chain(lambda i, d, hd: dv[i] + _bdot(dv[i], n8[i]))
    m = per_chain(lambda i, d, hd: _bdot(dv[i], off[i]))
    r1 = per_chain(lambda i, d, hd: _bdot(dv[i], rhs[i]))
    m2 = per_chain(lambda i, d, hd: _bdot(m[i], m[i]))
    x1 = per_chain(lambda i, d, hd: r1[i] - _bdot(m[i], r1[i]))
    uw = per_chain(lambda i, d, hd: x1[i] + _bdot(m2[i], x1[i]))

    s_old = per_chain(lambda i, d, hd: s_ref[i])
    ws = per_chain(lambda i, d, hd: _bdot(jnp.concatenate([uw[i][:, HEAD_DIM:], q_dec[i]], 0), s_old[i]))
    v_new = per_chain(lambda i, d, hd: (uw[i][:, :HEAD_DIM] - ws[i][:L]).astype(BF16))
    o = per_chain(lambda i, d, hd: ws[i][L:] + _bdot(attn[i], v_new[i]))
    kv = per_chain(lambda i, d, hd: lax.dot_general(
        k_dec[i].astype(BF16), v_new[i], (((0,), (0,)), ((), ())), preferred_element_type=F32))
    for i, (d, hd) in enumerate(chains):
        s_ref[i] = s_old[i] * jnp.exp(g_last[i]) + kv[i]
        (of_ref, ob_ref)[d][:, lanes(hd)] = o[i].astype(of_ref.dtype)


def _delta_scan(q, k, v, gb, n_lat, n_ctx, batch):
    t, cw = q.shape
    nlc, ncc = n_lat // C_CHUNK, n_ctx // C_CHUNK
    lat_blocks = batch * nlc

    def fwd_map(b, c):
        return jnp.where(c < ncc, lat_blocks + b * ncc + c, b * nlc + (c - ncc))

    def bwd_map(b, c):
        return jnp.where(c < ncc, lat_blocks + b * ncc + (ncc - 1 - c), b * nlc + (nlc - 1 - (c - ncc)))

    fspec = lambda w: pl.BlockSpec((C_CHUNK, w), lambda b, c: (fwd_map(b, c), 0))
    bspec = lambda w: pl.BlockSpec((C_CHUNK, w), lambda b, c: (bwd_map(b, c), 0))
    return pl.pallas_call(
        _delta_scan_kernel,
        grid=(batch, ncc + nlc),
        in_specs=[fspec(cw), fspec(cw), fspec(cw), fspec(LANES), bspec(cw), bspec(cw), bspec(cw), bspec(LANES)],
        out_specs=[fspec(cw), bspec(cw)],
        out_shape=[jax.ShapeDtypeStruct((t, cw), BF16)] * 2,
        scratch_shapes=[pltpu.VMEM((2 * C_HEADS, HEAD_DIM, HEAD_DIM), F32)],
        compiler_params=_cparams(("parallel", "arbitrary")),
        name="delta_scan",
    )(q, k, v, gb, q, k, v, gb)


def _delta_out_kernel(of_ref, ob_ref, z_ref, ng_ref, w_ref, h_ref, m_ref, lng_ref, lnb_ref, o_ref, y_ref, *, alpha):
    for hd in range(C_HEADS):
        sl = slice(hd * HEAD_DIM, (hd + 1) * HEAD_DIM)
        o = of_ref[:, sl].astype(F32) + ob_ref[:, sl].astype(F32)
        on = o * lax.rsqrt(jnp.mean(o * o, -1, keepdims=True) + EPS) * ng_ref[...]
        y_ref[:, sl] = (on * _silu(z_ref[:, sl])).astype(BF16)
    y = jnp.dot(y_ref[...], w_ref[...], preferred_element_type=F32)
    o_ref[...] = _residual_ln(h_ref[...], y, m_ref, G1, lng_ref, lnb_ref, alpha)


def _delta_out(o_f, o_b, proj, norm_g, w_out, h, mod, lng, lnb, tm, n_lat, batch, alpha):
    t, d = h.shape
    cw = C_HEADS * HEAD_DIM
    row = lambda w: pl.BlockSpec((tm, w), lambda i: (i, 0))
    return pl.pallas_call(
        functools.partial(_delta_out_kernel, alpha=alpha),
        grid=(t // tm,),
        in_specs=[
            row(cw), row(cw),
            pl.BlockSpec((tm, cw), lambda i: (i, 3)),
            pl.BlockSpec((1, HEAD_DIM), lambda i: (0, 0)),
            pl.BlockSpec((cw, d), lambda i: (0, 0)),
            row(d),
            _mod_spec(d, n_lat // tm, batch),
            pl.BlockSpec((1, d), lambda i: (0, 0)),
            pl.BlockSpec((1, d), lambda i: (0, 0)),
        ],
        out_specs=row(d),
        out_shape=jax.ShapeDtypeStruct((t, d), F32),
        scratch_shapes=[pltpu.VMEM((tm, cw), BF16)],
        compiler_params=_cparams(("parallel",)),
        name="delta_out_ln",
    )(o_f, o_b, proj, norm_g[None], w_out, h, mod, lng, lnb)


def _deltanet_layer(h, mod, w_in, conv_w, a_log, dt_bias, norm_g, w_out, lng, lnb, dims, alpha):
    batch, n_lat, n_ctx, tm = dims
    cw = C_HEADS * HEAD_DIM
    d = h.shape[1]
    n_gate = w_in.shape[1] - 4 * cw
    w = jnp.concatenate([w_in, jnp.zeros((d, LANES - n_gate), w_in.dtype)], 1).astype(BF16)
    proj, ab = _delta_proj(h, mod, w, tm, n_lat, batch)
    tmc = _pick_tile(256, n_lat, n_ctx)
    q, k, v, gb = _delta_prep(proj, ab, conv_w, a_log, dt_bias, tmc, n_lat, n_ctx, batch)
    o_f, o_b = _delta_scan(q, k, v, gb, n_lat, n_ctx, batch)
    return _delta_out(o_f, o_b, proj, norm_g, w_out.astype(BF16), h, mod, lng, lnb, tm, n_lat, batch, alpha)


PAGE = 16
PAGES_PER_BLOCK = 32


def _local_rows(tm):
    return TOP_K * tm + N_EXPERTS * PAGE


def _router_kernel(h_ref, m_ref, wr_ref, bias_ref, xs_ref, rt_ref, npg_ref):
    u = _modulate(h_ref[...], m_ref, SH2, SC2)
    tm = u.shape[0]
    u1, u2, _ = _split3(u)
    w1, w2, _ = _split3(wr_ref[...])
    nt = lambda a, b: lax.dot_general(a, b, (((1,), (1,)), ((), ())), preferred_element_type=F32)
    logits = nt(w1, u1) + nt(w1, u2) + nt(w2, u1)
    scores = jax.nn.sigmoid(logits)
    sel = scores + bias_ref[:, 0:1]
    per = N_EXPERTS // N_GROUPS
    best = None
    gidx = jnp.zeros((1, tm), jnp.int32)
    for g in range(N_GROUPS):
        r = [sel[g * per + j:g * per + j + 1, :] for j in range(per)]
        top2 = None
        for a in range(per):
            for b in range(a + 1, per):
                pair = r[a] + r[b]
                top2 = pair if top2 is None else jnp.maximum(top2, pair)
        if best is None:
            best = top2
        else:
            better = top2 > best
            gidx = jnp.where(better, g, gidx)
            best = jnp.where(better, top2, best)
    eio = lax.broadcasted_iota(jnp.int32, (N_EXPERTS, tm), 0)
    masked = jnp.where(eio // per == gidx, sel, -jnp.inf)
    m1 = jnp.max(masked, 0, keepdims=True)
    i1 = jnp.min(jnp.where(masked == m1, eio, N_EXPERTS), 0, keepdims=True)
    masked2 = jnp.where(eio == i1, -jnp.inf, masked)
    m2 = jnp.max(masked2, 0, keepdims=True)
    i2 = jnp.min(jnp.where(masked2 == m2, eio, N_EXPERTS), 0, keepdims=True)
    oh1 = eio == i1
    oh2 = eio == i2
    s1 = jnp.sum(jnp.where(oh1, scores, 0.0), 0, keepdims=True)
    s2 = jnp.sum(jnp.where(oh2, scores, 0.0), 0, keepdims=True)
    tot = s1 + s2
    oh = jnp.where(oh1, 1.0, jnp.where(oh2, 1.0, 0.0))
    ti = lax.broadcasted_iota(jnp.int32, (tm, tm), 0)
    tj = lax.broadcasted_iota(jnp.int32, (tm, tm), 1)
    before = jnp.where(ti < tj, 1.0, 0.0).astype(BF16)
    rank = jnp.dot(oh.astype(BF16), before, preferred_element_type=F32)
    cnt = jnp.sum(oh, 1, keepdims=True)
    npg = jnp.floor((cnt + (PAGE - 1)) * (1.0 / PAGE))
    ei = lax.broadcasted_iota(jnp.int32, (N_EXPERTS, N_EXPERTS), 0)
    ej = lax.broadcasted_iota(jnp.int32, (N_EXPERTS, N_EXPERTS), 1)
    lower = jnp.where(ej < ei, 1.0, 0.0).astype(BF16)
    npg_b = jnp.broadcast_to(npg, (N_EXPERTS, LANES))
    start_pg = jnp.dot(lower, npg_b.astype(BF16), preferred_element_type=F32)
    pos = start_pg[:, 0:1] * PAGE + rank
    p1 = jnp.sum(jnp.where(oh1, pos, 0.0), 0, keepdims=True)
    p2 = jnp.sum(jnp.where(oh2, pos, 0.0), 0, keepdims=True)
    rows = xs_ref.shape[0]
    rio = lax.broadcasted_iota(jnp.int32, (rows, tm), 0).astype(F32)
    perm = jnp.where(rio == p1, 1.0, jnp.where(rio == p2, 1.0, 0.0)).astype(BF16)
    xs_ref[...] = jnp.dot(perm, u.astype(BF16), preferred_element_type=F32).astype(BF16)
    rt_ref[...] = jnp.concatenate([s1 / tot, s2 / tot, p1, p2, jnp.zeros((SUBLANES - 4, tm), F32)], 0)
    npg_ref[...] = npg_b


def _router(h, mod, w_router, router_bias, tm, n_lat, batch):
    t, d = h.shape
    nb = t // tm
    rows = _local_rows(tm)
    wr = w_router.T.astype(F32)
    bias = jnp.broadcast_to(router_bias.astype(F32)[:, None], (N_EXPERTS, LANES))
    return pl.pallas_call(
        _router_kernel,
        grid=(nb,),
        in_specs=[
            pl.BlockSpec((tm, d), lambda i: (i, 0)),
            _mod_spec(d, n_lat // tm, batch),
            pl.BlockSpec((N_EXPERTS, d), lambda i: (0, 0)),
            pl.BlockSpec((N_EXPERTS, LANES), lambda i: (0, 0)),
        ],
        out_specs=[
            pl.BlockSpec((rows, d), lambda i: (i, 0)),
            pl.BlockSpec((SUBLANES, tm), lambda i: (0, i)),
            pl.BlockSpec((N_EXPERTS, LANES), lambda i: (i, 0)),
        ],
        out_shape=[
            jax.ShapeDtypeStruct((nb * rows, d), BF16),
            jax.ShapeDtypeStruct((SUBLANES, t), F32),
            jax.ShapeDtypeStruct((nb * N_EXPERTS, LANES), F32),
        ],
        compiler_params=_cparams(("parallel",)),
        name="moe_router",
    )(h, mod, wr, bias)


def _expert_kernel(pg_ref, be_ref, nu_ref, xs_hbm, wg_ref, wu_ref, wd_ref, ys_in_hbm, ys_hbm,
                   xbuf, ybuf, in_sem, out_sem, *, nblk):
    del ys_in_hbm
    i = pl.program_id(0)
    n_used = nu_ref[0]

    def page_copies(blk, buf, fn):
        for j in range(PAGES_PER_BLOCK):
            page = pg_ref[blk * PAGES_PER_BLOCK + j]

            @pl.when(page >= 0)
            def _():
                fn(page, buf, j)

    def in_copy(page, buf, j):
        return pltpu.make_async_copy(xs_hbm.at[pl.ds(pl.multiple_of(page * PAGE, PAGE), PAGE)],
                                     xbuf.at[buf, pl.ds(j * PAGE, PAGE)], in_sem.at[buf])

    def out_copy(page, buf, j):
        return pltpu.make_async_copy(ybuf.at[buf, pl.ds(j * PAGE, PAGE)],
                                     ys_hbm.at[pl.ds(pl.multiple_of(page * PAGE, PAGE), PAGE)], out_sem.at[buf])

    def fetch(blk, buf):
        page_copies(blk, buf, lambda page, buf, j: in_copy(page, buf, j).start())
        for j in range(PAGES_PER_BLOCK):
            @pl.when(pg_ref[blk * PAGES_PER_BLOCK + j] < 0)
            def _():
                xbuf[buf, pl.ds(j * PAGE, PAGE), :] = jnp.zeros((PAGE, xbuf.shape[2]), xbuf.dtype)

    def wait_out(blk, buf):
        page_copies(blk, buf, lambda page, buf, j: out_copy(page, buf, j).wait())

    buf = i % 2

    @pl.when(jnp.logical_and(i == 0, n_used > 0))
    def _():
        fetch(0, 0)

    @pl.when(i + 1 < n_used)
    def _():
        fetch(i + 1, 1 - buf)

    @pl.when(i < n_used)
    def _():
        page_copies(i, buf, lambda page, buf, j: in_copy(page, buf, j).wait())

        @pl.when(i >= 2)
        def _():
            wait_out(i - 2, buf)

        x = xbuf[buf]
        hid = _silu(_bdot(x, wg_ref[...])) * _bdot(x, wu_ref[...])
        ybuf[buf] = _bdot(hid, wd_ref[...]).astype(ybuf.dtype)
        page_copies(i, buf, lambda page, buf, j: out_copy(page, buf, j).start())

    @pl.when(i == nblk - 1)
    def _():
        for back in (2, 1):
            @pl.when(n_used - back >= 0)
            def _():
                wait_out(n_used - back, (n_used - back) % 2)


def _experts(xs, slot_page, block_e, n_used, e_gate, e_up, e_down, layer):
    n_rows, d = xs.shape
    de = e_gate.shape[-1]
    nblk = block_e.shape[0]
    rows = PAGES_PER_BLOCK * PAGE
    any_spec = pl.BlockSpec(memory_space=pl.ANY)
    wspec = lambda a, b: pl.BlockSpec((None, None, a, b), lambda i, pg, be, nu: (layer, be[i], 0, 0))
    grid_spec = pltpu.PrefetchScalarGridSpec(
        num_scalar_prefetch=3,
        grid=(nblk,),
        in_specs=[any_spec, wspec(d, de), wspec(d, de), wspec(de, d), any_spec],
        out_specs=any_spec,
        scratch_shapes=[
            pltpu.VMEM((2, rows, d), BF16),
            pltpu.VMEM((2, rows, d), BF16),
            pltpu.SemaphoreType.DMA((2,)),
            pltpu.SemaphoreType.DMA((2,)),
        ],
    )
    return pl.pallas_call(
        functools.partial(_expert_kernel, nblk=nblk),
        grid_spec=grid_spec,
        out_shape=jax.ShapeDtypeStruct((n_rows, d), BF16),
        input_output_aliases={7: 0},
        compiler_params=_cparams(("arbitrary",)),
        name="moe_experts",
    )(slot_page, block_e, n_used, xs, e_gate, e_up, e_down, jnp.zeros((n_rows, d), BF16))


def _moe_combine_kernel(ys_ref, rt_ref, h_ref, m_ref, lng_ref, lnb_ref, o_ref, *, alpha):
    tm = h_ref.shape[0]
    rows = ys_ref.shape[0]
    ti = lax.broadcasted_iota(jnp.int32, (tm, tm), 0)
    tj = lax.broadcasted_iota(jnp.int32, (tm, tm), 1)
    eye = jnp.where(ti == tj, 1.0, 0.0).astype(BF16)
    cols = sum(lax.dot_general(eye, piece, (((1,), (1,)), ((), ())), preferred_element_type=F32)
               for piece in _split3(rt_ref[...]))
    lane = lax.broadcasted_iota(jnp.int32, (tm, rows), 1).astype(F32)
    ys = ys_ref[...]
    y = None
    for k in range(TOP_K):
        sel = jnp.where(lane == cols[:, TOP_K + k:TOP_K + k + 1], 1.0, 0.0).astype(BF16)
        part = cols[:, k:k + 1] * jnp.dot(sel, ys, preferred_element_type=F32)
        y = part if y is None else y + part
    o_ref[...] = _residual_ln(h_ref[...], y, m_ref, G2, lng_ref, lnb_ref, alpha)


def _moe_combine(ys, rt, h, mod, lng, lnb, tm, n_lat, batch, alpha):
    t, d = h.shape
    rows = _local_rows(tm)
    row = pl.BlockSpec((tm, d), lambda i: (i, 0))
    return pl.pallas_call(
        functools.partial(_moe_combine_kernel, alpha=alpha),
        grid=(t // tm,),
        in_specs=[pl.BlockSpec((rows, d), lambda i: (i, 0)), pl.BlockSpec((SUBLANES, tm), lambda i: (0, i)),
                  row, _mod_spec(d, n_lat // tm, batch),
                  pl.BlockSpec((1, d), lambda i: (0, 0)), pl.BlockSpec((1, d), lambda i: (0, 0))],
        out_specs=row,
        out_shape=jax.ShapeDtypeStruct((t, d), F32),
        compiler_params=_cparams(("parallel",)),
        name="moe_combine_ln",
    )(ys, rt, h, mod, lng, lnb)


def _page_table(npg, pages_per_local):
    nb = npg.shape[0]
    ppb = PAGES_PER_BLOCK
    max_pages = nb * (pages_per_local - 1) + N_EXPERTS * (ppb - 1)
    nblk = -(-max_pages // ppb)
    lstart = jnp.cumsum(npg, 1) - npg
    pe = jnp.sum(npg, 0)
    pe_pad = (pe + ppb - 1) // ppb * ppb
    eend = jnp.cumsum(pe_pad)
    estart = eend - pe_pad
    cb_end = jnp.cumsum(npg, 0).T
    s = jnp.arange(nblk * ppb, dtype=jnp.int32)
    e_s = jnp.minimum(jnp.sum((eend[None, :] <= s[:, None]).astype(jnp.int32), 1), N_EXPERTS - 1)
    j = s - estart[e_s]
    valid = j < pe[e_s]
    ends = cb_end[e_s]
    b_s = jnp.minimum(jnp.sum((ends <= j[:, None]).astype(jnp.int32), 1), nb - 1)
    run_first = jnp.take_along_axis(ends, b_s[:, None], 1)[:, 0] - npg[b_s, e_s]
    page = b_s * pages_per_local + lstart[b_s, e_s] + (j - run_first)
    slot_page = jnp.where(valid, page, -1).astype(jnp.int32)
    block_e = e_s[::ppb].astype(jnp.int32)
    n_used = (eend[-1:] // ppb).astype(jnp.int32)
    return slot_page, block_e, n_used


def _moe_layer(h, mod, w_router, router_bias, e_gate, e_up, e_down, layer, lng, lnb, dims, alpha):
    batch, n_lat, n_ctx, tm = dims
    xs, rt, npg = _router(h, mod, w_router, router_bias, tm, n_lat, batch)
    npg = npg[:, 0].reshape(-1, N_EXPERTS).astype(jnp.int32)
    slot_page, block_e, n_used = _page_table(npg, _local_rows(tm) // PAGE)
    ys = _experts(xs, slot_page, block_e, n_used, e_gate, e_up, e_down, layer)
    return _moe_combine(ys, rt, h, mod, lng, lnb, tm, n_lat, batch, alpha)


def kernel(x, c, ctx, c_ctx, w_ada, b_ada, ln_g, ln_b, a_wqkv, a_qg, a_kg, a_wo, b_win, b_vg, b_vb, b_ws, b_bs,
           b_wout, c_win, c_conv, c_alog, c_dtb, c_ng, c_wout, w_router, router_bias, e_gate, e_up, e_down):
    batch, n_lat, d = x.shape
    n_ctx = ctx.shape[1]
    depth = w_ada.shape[0]
    alpha = (2 * depth) ** 0.25
    assert batch + 1 <= SUBLANES
    tm = _pick_tile(512, n_lat, batch * n_ctx)
    dims = (batch, n_lat, n_ctx, tm)

    h = jnp.concatenate([x.reshape(batch * n_lat, d), ctx.reshape(batch * n_ctx, d)], 0).astype(F32)
    cond = jnp.concatenate([c, c_ctx[None], jnp.zeros((SUBLANES - batch - 1, d), c.dtype)], 0).astype(F32)
    ada = _ada_table(cond, w_ada, b_ada)
    mods = jnp.pad(ada.reshape(depth, SUBLANES, 6, d), ((0, 0), (0, 0), (0, SUBLANES - 6), (0, 0)))
    rope = _rope_tables(n_lat, tm)

    for i in range(depth):
        m, j = i % N_MIXERS, i // N_MIXERS
        mod = mods[i]
        lng1, lnb1 = ln_g[i, 0][None], ln_b[i, 0][None]
        lng2, lnb2 = ln_g[i, 1][None], ln_b[i, 1][None]
        if m == 0:
            h = _attention_layer(h, mod, a_wqkv[j], a_qg[j], a_kg[j], a_wo[j], lng1, lnb1, rope, dims, alpha)
        elif m == 1:
            h = _gmlp_layer(h, mod, b_win[j], b_vg[j], b_vb[j], b_ws[j], b_bs[j], b_wout[j], lng1, lnb1, dims, alpha)
        else:
            h = _deltanet_layer(h, mod, c_win[j], c_conv[j], c_alog[j], c_dtb[j], c_ng[j], c_wout[j],
                                lng1, lnb1, dims, alpha)
        h = _moe_layer(h, mod, w_router, router_bias, e_gate, e_up, e_down, i, lng2, lnb2, dims, alpha)
    return h[:batch * n_lat].reshape(batch, n_lat, d).astype(x.dtype)
```

```python
import functools
import math

import jax
import jax.numpy as jnp
from jax import lax
from jax.experimental import pallas as pl
from jax.experimental.pallas import tpu as pltpu

F32 = jnp.float32
BF16 = jnp.bfloat16

GRID_W = 64
A_HEADS = 8
A_KV_HEADS = 2
A_GROUP = A_HEADS // A_KV_HEADS
HEAD_DIM = 128
ROPE_THETA = 10000.0
B_CHUNK = 128
B_GROUPS = 8
C_HEADS = 8
C_CONV = 3
C_CHUNK = 64
N_EXPERTS = 16
N_GROUPS = 4
TOP_K = 2
EPS = 1e-6
N_MIXERS = 3

LANES = 128
SUBLANES = 8
VMEM_LIMIT_BYTES = 56 * 1024 * 1024
LOG2E = 1.4426950408889634


def _cparams(sem, vmem=VMEM_LIMIT_BYTES):
    return pltpu.CompilerParams(dimension_semantics=sem, vmem_limit_bytes=vmem)


def _pick_tile(cap, *sizes):
    t = cap
    while t > SUBLANES and any(s % t for s in sizes):
        t //= 2
    assert all(s % t == 0 for s in sizes), (cap, sizes)
    return t


def _layer_norm(x, g, b):
    mu = jnp.mean(x, -1, keepdims=True)
    xc = x - mu
    var = jnp.mean(xc * xc, -1, keepdims=True)
    return xc * lax.rsqrt(var + EPS) * g + b


def _silu(x):
    return x * jax.nn.sigmoid(x)


def _bdot(a, b):
    return jnp.dot(a.astype(BF16), b.astype(BF16), preferred_element_type=F32)


def _bdot_nt(a, b):
    return lax.dot_general(a.astype(BF16), b.astype(BF16), (((1,), (1,)), ((), ())), preferred_element_type=F32)


def _split3(x):
    x1 = x.astype(BF16)
    r = x - x1.astype(F32)
    x2 = r.astype(BF16)
    x3 = (r - x2.astype(F32)).astype(BF16)
    return x1, x2, x3


def _ada_kernel(c_ref, w_ref, b_ref, o_ref):
    s = _silu(c_ref[...])
    o_ref[...] = jnp.dot(s, w_ref[...], preferred_element_type=F32, precision=lax.Precision.HIGHEST) + b_ref[...]


def _ada_table(cond, w_ada, b_ada):
    depth, d, d6 = w_ada.shape
    tn = _pick_tile(1536, d6)
    return pl.pallas_call(
        _ada_kernel,
        grid=(depth, d6 // tn),
        in_specs=[
            pl.BlockSpec((SUBLANES, d), lambda i, j: (0, 0)),
            pl.BlockSpec((None, d, tn), lambda i, j: (i, 0, j)),
            pl.BlockSpec((None, 1, tn), lambda i, j: (i, 0, j)),
        ],
        out_specs=pl.BlockSpec((None, SUBLANES, tn), lambda i, j: (i, 0, j)),
        out_shape=jax.ShapeDtypeStruct((depth, SUBLANES, d6), F32),
        compiler_params=_cparams(("arbitrary", "arbitrary")),
        name="ada_table",
    )(cond, w_ada, b_ada.reshape(depth, 1, d6))


SH1, SC1, G1, SH2, SC2, G2 = range(6)


def _mod_spec(d, blocks_per_batch, batch):
    return pl.BlockSpec((None, SUBLANES, d), lambda i, *_: (jnp.minimum(i // blocks_per_batch, batch), 0, 0))


def _modulate(h, m_ref, sh, sc):
    return h * (1.0 + m_ref[sc:sc + 1, :]) + m_ref[sh:sh + 1, :]


def _residual_ln(h, y, m_ref, gate, lng_ref, lnb_ref, alpha):
    return _layer_norm(alpha * h + m_ref[gate:gate + 1, :] * y, lng_ref[...], lnb_ref[...])


def _attn_proj_kernel(h_ref, m_ref, w_ref, qg_ref, kg_ref, cos_ref, sin_ref, q_ref, k_ref, v_ref):
    u = _modulate(h_ref[...], m_ref, SH1, SC1)
    qkv = _bdot(u, w_ref[...])
    cos = cos_ref[...]
    sin = sin_ref[...]
    nq = A_HEADS * HEAD_DIM
    nk = A_KV_HEADS * HEAD_DIM

    def norm_rope(xh, g):
        xn = xh * lax.rsqrt(jnp.mean(xh * xh, -1, keepdims=True) + EPS) * g
        return xn * cos + pltpu.roll(xn, HEAD_DIM // 2, 1) * sin

    for hd in range(A_HEADS):
        sl = slice(hd * HEAD_DIM, (hd + 1) * HEAD_DIM)
        q_ref[:, sl] = norm_rope(qkv[:, sl], qg_ref[...]).astype(BF16)
    for hd in range(A_KV_HEADS):
        sl = slice(hd * HEAD_DIM, (hd + 1) * HEAD_DIM)
        k_ref[:, sl] = norm_rope(qkv[:, nq + hd * HEAD_DIM:nq + (hd + 1) * HEAD_DIM], kg_ref[...]).astype(BF16)
    v_ref[...] = qkv[:, nq + nk:].astype(BF16)


def _attn_proj(h, mod, w_qkv, qg, kg, cos_t, sin_t, tm, n_lat, batch):
    t, d = h.shape
    nq = A_HEADS * HEAD_DIM
    nk = A_KV_HEADS * HEAD_DIM
    bpb = n_lat // tm

    def rope_map(i):
        return (jnp.where(i < batch * bpb, i % bpb, bpb), 0)

    return pl.pallas_call(
        _attn_proj_kernel,
        grid=(t // tm,),
        in_specs=[
            pl.BlockSpec((tm, d), lambda i: (i, 0)),
            _mod_spec(d, bpb, batch),
            pl.BlockSpec((d, nq + 2 * nk), lambda i: (0, 0)),
            pl.BlockSpec((1, HEAD_DIM), lambda i: (0, 0)),
            pl.BlockSpec((1, HEAD_DIM), lambda i: (0, 0)),
            pl.BlockSpec((tm, HEAD_DIM), rope_map),
            pl.BlockSpec((tm, HEAD_DIM), rope_map),
        ],
        out_specs=[
            pl.BlockSpec((tm, nq), lambda i: (i, 0)),
            pl.BlockSpec((tm, nk), lambda i: (i, 0)),
            pl.BlockSpec((tm, nk), lambda i: (i, 0)),
        ],
        out_shape=[
            jax.ShapeDtypeStruct((t, nq), BF16),
            jax.ShapeDtypeStruct((t, nk), BF16),
            jax.ShapeDtypeStruct((t, nk), BF16),
        ],
        compiler_params=_cparams(("parallel",)),
        name="attn_proj",
    )(h, mod, w_qkv, qg, kg, cos_t, sin_t)


def _flash_kernel(q_ref, k_ref, v_ref, o_ref, m_ref, l_ref, acc_ref, *, nchunk, tk):
    m_ref[...] = jnp.full(m_ref.shape, -jnp.inf, F32)
    l_ref[...] = jnp.zeros(l_ref.shape, F32)
    acc_ref[...] = jnp.zeros(acc_ref.shape, F32)
    nslab = tk // LANES

    def body(j, carry):
        off = pl.multiple_of(j * tk, tk)
        kj = k_ref[pl.ds(off, tk), :]
        vj = v_ref[pl.ds(off, tk), :]

        def scores(g):
            qg = q_ref[:, g * HEAD_DIM:(g + 1) * HEAD_DIM]
            return lax.dot_general(qg, kj, (((1,), (1,)), ((), ())), preferred_element_type=F32)

        sc_next = scores(0)
        for g in range(A_GROUP):
            sc = sc_next
            if g + 1 < A_GROUP:
                sc_next = scores(g + 1)
            slabs = [sc[:, c * LANES:(c + 1) * LANES] for c in range(nslab)]
            mc = slabs[0]
            for sl in slabs[1:]:
                mc = jnp.maximum(mc, sl)
            m_old = m_ref[g]
            m_new = jnp.maximum(m_old, jnp.max(mc, -1, keepdims=True))
            alpha = jnp.exp2(m_old - m_new)
            ps = [jnp.exp2(sl - m_new) for sl in slabs]
            lsum = ps[0]
            for x in ps[1:]:
                lsum = lsum + x
            p = jnp.concatenate([x.astype(BF16) for x in ps], axis=1)
            l_ref[g] = alpha * l_ref[g] + lsum
            acc_ref[g] = alpha * acc_ref[g] + jnp.dot(p, vj, preferred_element_type=F32)
            m_ref[g] = m_new
        return carry

    lax.fori_loop(0, nchunk, body, 0, unroll=next(u for u in (3, 2, 1) if nchunk % u == 0))
    for g in range(A_GROUP):
        out = acc_ref[g] / jnp.sum(l_ref[g], -1, keepdims=True)
        o_ref[:, g * HEAD_DIM:(g + 1) * HEAD_DIM] = out.astype(o_ref.dtype)


def _key_tile(n):
    for tk in (1408, 768, 512, 384, 256, 128):
        if n % tk == 0:
            return tk
    raise ValueError(n)


def _flash(q, k, v, *, batch, q_row0, q_len, tq):
    gw = A_GROUP * HEAD_DIM
    nqb = q_len // tq
    n_keys = k.shape[1]
    tk = _key_tile(n_keys)
    kv_spec = pl.BlockSpec((None, n_keys, HEAD_DIM), lambda b, kv, i: (b, 0, kv))
    return pl.pallas_call(
        functools.partial(_flash_kernel, nchunk=n_keys // tk, tk=tk),
        grid=(batch, A_KV_HEADS, nqb),
        in_specs=[pl.BlockSpec((tq, gw), lambda b, kv, i: (q_row0 // tq + b * nqb + i, kv)), kv_spec, kv_spec],
        out_specs=pl.BlockSpec((tq, gw), lambda b, kv, i: (b * nqb + i, kv)),
        out_shape=jax.ShapeDtypeStruct((batch * q_len, A_HEADS * HEAD_DIM), BF16),
        scratch_shapes=[pltpu.VMEM((A_GROUP, tq, LANES), F32)] * 2 + [pltpu.VMEM((A_GROUP, tq, HEAD_DIM), F32)],
        compiler_params=_cparams(("parallel", "parallel", "arbitrary")),
        name="flash_attn",
    )(q, k, v)


def _out_proj_kernel(a_ref, w_ref, h_ref, m_ref, lng_ref, lnb_ref, o_ref, *, alpha):
    y = jnp.dot(a_ref[...], w_ref[...], preferred_element_type=F32)
    o_ref[...] = _residual_ln(h_ref[...], y, m_ref, G1, lng_ref, lnb_ref, alpha)


def _out_proj(a, w, h, mod, lng, lnb, tm, n_lat, batch, alpha):
    t, d = h.shape
    ka = a.shape[1]
    return pl.pallas_call(
        functools.partial(_out_proj_kernel, alpha=alpha),
        grid=(t // tm,),
        in_specs=[
            pl.BlockSpec((tm, ka), lambda i: (i, 0)),
            pl.BlockSpec((ka, d), lambda i: (0, 0)),
            pl.BlockSpec((tm, d), lambda i: (i, 0)),
            _mod_spec(d, n_lat // tm, batch),
            pl.BlockSpec((1, d), lambda i: (0, 0)),
            pl.BlockSpec((1, d), lambda i: (0, 0)),
        ],
        out_specs=pl.BlockSpec((tm, d), lambda i: (i, 0)),
        out_shape=jax.ShapeDtypeStruct((t, d), F32),
        compiler_params=_cparams(("parallel",)),
        name="out_proj_ln",
    )(a, w, h, mod, lng, lnb)


def _rope_tables(n, tm):
    rows = n // GRID_W
    half = HEAD_DIM // 2
    inv = 1.0 / (ROPE_THETA ** (jnp.arange(0, half, 2, dtype=F32) / half))
    r = jnp.broadcast_to(jnp.arange(rows, dtype=F32)[:, None], (rows, GRID_W)).reshape(-1)
    col = jnp.broadcast_to(jnp.arange(GRID_W, dtype=F32)[None, :], (rows, GRID_W)).reshape(-1)
    ang = jnp.concatenate([r[:, None] * inv, col[:, None] * inv], -1)
    cos, sin = jnp.cos(ang), jnp.sin(ang)
    cos_t = jnp.concatenate([jnp.concatenate([cos, cos], -1), jnp.ones((tm, HEAD_DIM), F32)], 0)
    sin_t = jnp.concatenate([jnp.concatenate([-sin, sin], -1), jnp.zeros((tm, HEAD_DIM), F32)], 0)
    return cos_t, sin_t


def _deinterleave_heads(w, n_heads):
    lead = w.shape[:-1]
    w = w.reshape(*lead, n_heads, HEAD_DIM // 2, 2)
    return jnp.swapaxes(w, -1, -2).reshape(*lead, n_heads * HEAD_DIM)


def _attention_layer(h, mod, w_qkv, q_g, k_g, w_o, lng, lnb, rope, dims, alpha):
    batch, n_lat, n_ctx, tm = dims
    nq = A_HEADS * HEAD_DIM
    nk = A_KV_HEADS * HEAD_DIM
    w = jnp.concatenate([
        _deinterleave_heads(w_qkv[:, :nq], A_HEADS),
        _deinterleave_heads(w_qkv[:, nq:nq + nk], A_KV_HEADS),
        w_qkv[:, nq + nk:]], -1).astype(BF16)
    qg = _deinterleave_heads(q_g[None, :], 1) * (HEAD_DIM ** -0.5 * LOG2E)
    kg = _deinterleave_heads(k_g[None, :], 1)
    q, k, v = _attn_proj(h, mod, w, qg, kg, rope[0], rope[1], tm, n_lat, batch)
    lat0 = batch * n_lat
    kvw = A_KV_HEADS * HEAD_DIM
    k_ctx, v_ctx = k[lat0:].reshape(batch, n_ctx, kvw), v[lat0:].reshape(batch, n_ctx, kvw)
    k_all = jnp.concatenate([k_ctx, k[:lat0].reshape(batch, n_lat, kvw)], 1)
    v_all = jnp.concatenate([v_ctx, v[:lat0].reshape(batch, n_lat, kvw)], 1)
    o_lat = _flash(q, k_all, v_all, batch=batch, q_row0=0, q_len=n_lat, tq=_pick_tile(512, n_lat))
    o_ctx = _flash(q, k_ctx, v_ctx, batch=batch, q_row0=lat0, q_len=n_ctx, tq=_pick_tile(256, n_ctx))
    o = jnp.concatenate([o_lat, o_ctx], 0)
    return _out_proj(o, w_o.astype(BF16), h, mod, lng, lnb, tm, n_lat, batch, alpha)


def _gmlp_kernel(h_ref, m_ref, win_ref, vg_ref, vb_ref, ws_ref, bs_ref, wout_ref, lng_ref, lnb_ref, o_ref,
                 gated_ref, *, alpha, width):
    h = h_ref[...]
    u = _modulate(h, m_ref, SH1, SC1)
    z = jax.nn.gelu(_bdot(u, win_ref[...]))
    uu = z[:, :width]
    v = _layer_norm(z[:, width:], vg_ref[...], vb_ref[...]).astype(BF16)
    tm = h.shape[0]
    gw = width // B_GROUPS
    for c in range(tm // B_CHUNK):
        rs = slice(c * B_CHUNK, (c + 1) * B_CHUNK)
        for g in range(B_GROUPS):
            cs = slice(g * gw, (g + 1) * gw)
            mixed = jnp.dot(ws_ref[g], v[rs, cs], preferred_element_type=F32) + bs_ref[g]
            gated_ref[rs, cs] = (uu[rs, cs] * mixed).astype(BF16)
    y = jnp.dot(gated_ref[...], wout_ref[...], preferred_element_type=F32)
    o_ref[...] = _residual_ln(h, y, m_ref, G1, lng_ref, lnb_ref, alpha)


def _gmlp_layer(h, mod, w_in, v_g, v_b, w_s, b_s, w_out, lng, lnb, dims, alpha):
    batch, n_lat, n_ctx, tm = dims
    t, d = h.shape
    width = w_out.shape[0]
    gw = width // B_GROUPS
    assert tm % B_CHUNK == 0 and n_ctx % B_CHUNK == 0
    bs = jnp.broadcast_to(b_s[:, :, None], (B_GROUPS, B_CHUNK, gw)).astype(F32)
    const = lambda *shape: pl.BlockSpec(shape, lambda i: (0,) * len(shape))
    return pl.pallas_call(
        functools.partial(_gmlp_kernel, alpha=alpha, width=width),
        grid=(t // tm,),
        in_specs=[
            pl.BlockSpec((tm, d), lambda i: (i, 0)),
            _mod_spec(d, n_lat // tm, batch),
            const(d, 2 * width), const(1, width), const(1, width),
            const(B_GROUPS, B_CHUNK, B_CHUNK), const(B_GROUPS, B_CHUNK, gw),
            const(width, d), const(1, d), const(1, d),
        ],
        out_specs=pl.BlockSpec((tm, d), lambda i: (i, 0)),
        out_shape=jax.ShapeDtypeStruct((t, d), F32),
        scratch_shapes=[pltpu.VMEM((tm, width), BF16)],
        compiler_params=_cparams(("parallel",)),
        name="gmlp_layer",
    )(h, mod, w_in.astype(BF16), v_g[None], v_b[None], w_s.astype(BF16), bs, w_out.astype(BF16), lng, lnb)


def _delta_proj_kernel(h_ref, m_ref, w_ref, wg_ref, o_ref, og_ref, *, tn):
    u = _modulate(h_ref[...], m_ref, SH1, SC1).astype(BF16)
    for j in range(w_ref.shape[1] // tn):
        o_ref[:, j * tn:(j + 1) * tn] = jnp.dot(u, w_ref[:, j * tn:(j + 1) * tn], preferred_element_type=F32)
    og_ref[...] = jnp.dot(u, wg_ref[...], preferred_element_type=F32)


def _delta_proj(h, mod, w, tm, n_lat, batch):
    t, d = h.shape
    ncol = w.shape[1] - LANES
    return pl.pallas_call(
        functools.partial(_delta_proj_kernel, tn=_pick_tile(1024, ncol)),
        grid=(t // tm,),
        in_specs=[
            pl.BlockSpec((tm, d), lambda i: (i, 0)),
            _mod_spec(d, n_lat // tm, batch),
            pl.BlockSpec((d, ncol), lambda i: (0, 0)),
            pl.BlockSpec((d, LANES), lambda i: (0, 0)),
        ],
        out_specs=[pl.BlockSpec((tm, ncol), lambda i: (i, 0)), pl.BlockSpec((tm, LANES), lambda i: (i, 0))],
        out_shape=[jax.ShapeDtypeStruct((t, ncol), F32), jax.ShapeDtypeStruct((t, LANES), F32)],
        compiler_params=_cparams(("parallel",)),
        name="delta_proj",
    )(h, mod, w[:, :ncol], w[:, ncol:])


def _delta_prep_kernel(x_ref, xp_ref, xn_ref, ab_ref, cw_ref, ga_ref, dtb_ref, q_ref, k_ref, v_ref, gb_ref,
                       *, tm, n_lat_rows, n_lat, n_ctx, cw):
    i = pl.program_id(0)
    row0 = i * tm
    in_lat = row0 < n_lat_rows
    pos = jnp.where(in_lat, row0 % n_lat, (row0 - n_lat_rows) % n_ctx)
    seq = jnp.where(in_lat, n_lat, n_ctx)
    has_prev = (pos != 0).astype(F32)
    has_next = (pos + tm != seq).astype(F32)
    rows = lax.broadcasted_iota(jnp.int32, (tm, HEAD_DIM), 0)
    first = rows == 0
    last = rows == tm - 1
    heads = cw // HEAD_DIM
    for part, out_ref in enumerate((q_ref, k_ref, v_ref)):
        for hd in range(heads):
            sl = slice(part * cw + hd * HEAD_DIM, part * cw + (hd + 1) * HEAD_DIM)
            x = x_ref[:, sl]
            prev_row = xp_ref[SUBLANES - 1:SUBLANES, sl] * has_prev
            next_row = xn_ref[0:1, sl] * has_next
            x_prev = jnp.where(first, prev_row, pltpu.roll(x, 1, 0))
            x_next = jnp.where(last, next_row, pltpu.roll(x, tm - 1, 0))
            y = _silu(x_prev * cw_ref[0:1, sl] + x * cw_ref[1:2, sl] + x_next * cw_ref[2:3, sl])
            if part < 2:
                y = y * lax.rsqrt(jnp.sum(y * y, -1, keepdims=True) + EPS)
            if part == 0:
                y = y * (HEAD_DIM ** -0.5)
            out_ref[:, hd * HEAD_DIM:(hd + 1) * HEAD_DIM] = y.astype(out_ref.dtype)
    ab = ab_ref[...]
    xg = ab + dtb_ref[...]
    softplus = jnp.maximum(xg, 0.0) + jnp.log1p(jnp.exp(-jnp.abs(xg)))
    lane = lax.broadcasted_iota(jnp.int32, ab.shape, 1)
    gb_ref[...] = jnp.where(lane < 2 * C_HEADS, ga_ref[...] * softplus, jax.nn.sigmoid(ab))


def _delta_prep(proj, ab, conv_w, a_log, dt_bias, tm, n_lat, n_ctx, batch):
    t = proj.shape[0]
    cw = C_HEADS * HEAD_DIM
    r8 = tm // SUBLANES
    nb8 = t // SUBLANES
    ga = jnp.zeros((1, LANES), F32).at[0, :2 * C_HEADS].set(-jnp.exp(a_log.astype(F32)).reshape(-1))
    dtb = jnp.zeros((1, LANES), F32).at[0, :2 * C_HEADS].set(dt_bias.astype(F32).reshape(-1))
    cwp = jnp.zeros((SUBLANES, 3 * cw), F32).at[:C_CONV].set(conv_w)
    return pl.pallas_call(
        functools.partial(_delta_prep_kernel, tm=tm, n_lat_rows=batch * n_lat, n_lat=n_lat, n_ctx=n_ctx, cw=cw),
        grid=(t // tm,),
        in_specs=[
            pl.BlockSpec((tm, 3 * cw), lambda i: (i, 0)),
            pl.BlockSpec((SUBLANES, 3 * cw), lambda i: (jnp.maximum(i * r8 - 1, 0), 0)),
            pl.BlockSpec((SUBLANES, 3 * cw), lambda i: (jnp.minimum((i + 1) * r8, nb8 - 1), 0)),
            pl.BlockSpec((tm, LANES), lambda i: (i, 0)),
            pl.BlockSpec((SUBLANES, 3 * cw), lambda i: (0, 0)),
            pl.BlockSpec((1, LANES), lambda i: (0, 0)),
            pl.BlockSpec((1, LANES), lambda i: (0, 0)),
        ],
        out_specs=[pl.BlockSpec((tm, cw), lambda i: (i, 0))] * 3 + [pl.BlockSpec((tm, LANES), lambda i: (i, 0))],
        out_shape=[jax.ShapeDtypeStruct((t, cw), BF16)] * 3 + [jax.ShapeDtypeStruct((t, LANES), F32)],
        compiler_params=_cparams(("parallel",)),
        name="delta_prep",
    )(proj, proj, proj, ab, cwp, ga, dtb)


DELTA_BLOCK = 16


def _delta_scan_kernel(*refs):
    nst = 2
    in_refs = [refs[4 * st:4 * st + 4] for st in range(nst)]
    out_refs = refs[4 * nst:5 * nst]
    s_ref = refs[5 * nst]
    c = pl.program_id(1)

    @pl.when(c == 0)
    def _():
        s_ref[...] = jnp.zeros(s_ref.shape, F32)

    L = C_CHUNK
    ri = lax.broadcasted_iota(jnp.int32, (L, L), 0)
    ci = lax.broadcasted_iota(jnp.int32, (L, L), 1)
    eye = (ri == ci).astype(F32)
    blk_mask = (ri // DELTA_BLOCK) == (ci // DELTA_BLOCK)
    dirs = []
    for st, (q_ref, k_ref, v_ref, g_ref) in enumerate(in_refs):
        direction = st % 2
        tri, strict = (ri >= ci, ri > ci) if direction == 0 else (ri <= ci, ri < ci)
        tri_b = tri.astype(BF16)
        gb = g_ref[...]
        g1, g2, g3 = _split3(gb)
        gc = (jnp.dot(tri_b, g1, preferred_element_type=F32) + jnp.dot(tri_b, g2, preferred_element_type=F32)
              + jnp.dot(tri_b, g3, preferred_element_type=F32))
        dirs.append((q_ref, k_ref, v_ref, gb, gc, gc.T, tri, strict, L - 1 if direction == 0 else 0))

    chains = [(st, hd) for st in range(nst) for hd in range(C_HEADS)]

    def gate_lane(d, hd):
        return (d % 2) * C_HEADS + hd
    nch = len(chains)

    def per_chain(fn):
        return [fn(i, *chains[i]) for i in range(nch)]

    def lanes(hd):
        return slice(hd * HEAD_DIM, (hd + 1) * HEAD_DIM)

    kh = per_chain(lambda i, d, hd: dirs[d][1][:, lanes(hd)])
    qh = per_chain(lambda i, d, hd: dirs[d][0][:, lanes(hd)])
    kq = per_chain(lambda i, d, hd: _bdot_nt(jnp.concatenate([kh[i], qh[i]], 0), kh[i]))
    g_col = per_chain(lambda i, d, hd: dirs[d][4][:, gate_lane(d, hd):gate_lane(d, hd) + 1])
    g_row = per_chain(lambda i, d, hd: dirs[d][5][gate_lane(d, hd):gate_lane(d, hd) + 1, :])
    beta = per_chain(lambda i, d, hd: dirs[d][3][:, 2 * C_HEADS + gate_lane(d, hd):2 * C_HEADS + gate_lane(d, hd) + 1])
    g_last = per_chain(lambda i, d, hd: g_col[i][dirs[d][8]:dirs[d][8] + 1, :])
    decay = per_chain(lambda i, d, hd: jnp.exp(jnp.where(dirs[d][6], g_col[i] - g_row[i], -jnp.inf)))
    a_mat = per_chain(lambda i, d, hd: jnp.where(dirs[d][7], kq[i][:L] * beta[i] * decay[i], 0.0))
    attn = per_chain(lambda i, d, hd: kq[i][L:] * decay[i])
    eg = per_chain(lambda i, d, hd: jnp.exp(g_col[i]))
    rhs = per_chain(lambda i, d, hd: jnp.concatenate(
        [dirs[d][2][:, lanes(hd)].astype(F32) * beta[i], kh[i].astype(F32) * (beta[i] * eg[i])], 1))
    q_dec = per_chain(lambda i, d, hd: qh[i].astype(F32) * eg[i])
    k_dec = per_chain(lambda i, d, hd: kh[i].astype(F32) * jnp.exp(g_last[i] - g_col[i]))

    n = per_chain(lambda i, d, hd: jnp.where(blk_mask, a_mat[i], 0.0))
    off = per_chain(lambda i, d, hd: a_mat[i] - n[i])
    n2 = per_chain(lambda i, d, hd: _bdot(n[i], n[i]))
    n4 = per_chain(lambda i, d, hd: _bdot(n2[i], n2[i]))
    dv = per_chain(lambda i, d, hd: eye - n[i] + n2[i] - _bdot(n[i], n2[i]))
    n8 = per_chain(lambda i, d, hd: _bdot(n4[i], n4[i]))
    dv = per_chain(lambda i, d, hd: dv[i] + _bdot(dv[i], n4[i]))
    dv = per_chain(lambda i, d, hd: dv[i] + _bdot(dv[i], n8[i]))
    m = per_chain(lambda i, d, hd: _bdot(dv[i], off[i]))
    r1 = per_chain(lambda i, d, hd: _bdot(dv[i], rhs[i]))
    m2 = per_chain(lambda i, d, hd: _bdot(m[i], m[i]))
    x1 = per_chain(lambda i, d, hd: r1[i] - _bdot(m[i], r1[i]))
    uw = per_chain(lambda i, d, hd: x1[i] + _bdot(m2[i], x1[i]))

    s_old = per_chain(lambda i, d, hd: s_ref[i])
    ws = per_chain(lambda i, d, hd: _bdot(jnp.concatenate([uw[i][:, HEAD_DIM:], q_dec[i]], 0), s_old[i]))
    v_new = per_chain(lambda i, d, hd: (uw[i][:, :HEAD_DIM] - ws[i][:L]).astype(BF16))
    o = per_chain(lambda i, d, hd: ws[i][L:] + _bdot(attn[i], v_new[i]))
    kv = per_chain(lambda i, d, hd: lax.dot_general(
        k_dec[i].astype(BF16), v_new[i], (((0,), (0,)), ((), ())), preferred_element_type=F32))
    for i, (d, hd) in enumerate(chains):
        s_ref[i] = s_old[i] * jnp.exp(g_last[i]) + kv[i]
        out_refs[d][:, lanes(hd)] = o[i].astype(out_refs[d].dtype)


def _delta_scan(q, k, v, gb, n_lat, n_ctx, batch):
    t, cw = q.shape
    nlc, ncc = n_lat // C_CHUNK, n_ctx // C_CHUNK
    lat_blocks = batch * nlc

    def fwd_map(b, c):
        return jnp.where(c < ncc, lat_blocks + b * ncc + c, b * nlc + (c - ncc))

    def bwd_map(b, c):
        return jnp.where(c < ncc, lat_blocks + b * ncc + (ncc - 1 - c), b * nlc + (nlc - 1 - (c - ncc)))

    fspec = lambda w: pl.BlockSpec((C_CHUNK, w), lambda b, c: (fwd_map(b, c), 0))
    bspec = lambda w: pl.BlockSpec((C_CHUNK, w), lambda b, c: (bwd_map(b, c), 0))
    return pl.pallas_call(
        _delta_scan_kernel,
        grid=(batch, ncc + nlc),
        in_specs=[fspec(cw), fspec(cw), fspec(cw), fspec(LANES), bspec(cw), bspec(cw), bspec(cw), bspec(LANES)],
        out_specs=[fspec(cw), bspec(cw)],
        out_shape=[jax.ShapeDtypeStruct((t, cw), BF16)] * 2,
        scratch_shapes=[pltpu.VMEM((2 * C_HEADS, HEAD_DIM, HEAD_DIM), F32)],
        compiler_params=_cparams(("parallel", "arbitrary")),
        name="delta_scan",
    )(q, k, v, gb, q, k, v, gb)


def _delta_out_kernel(of_ref, ob_ref, z_ref, ng_ref, w_ref, h_ref, m_ref, lng_ref, lnb_ref, o_ref, y_ref, *, alpha):
    for hd in range(C_HEADS):
        sl = slice(hd * HEAD_DIM, (hd + 1) * HEAD_DIM)
        o = of_ref[:, sl].astype(F32) + ob_ref[:, sl].astype(F32)
        on = o * lax.rsqrt(jnp.mean(o * o, -1, keepdims=True) + EPS) * ng_ref[...]
        y_ref[:, sl] = (on * _silu(z_ref[:, sl])).astype(BF16)
    y = jnp.dot(y_ref[...], w_ref[...], preferred_element_type=F32)
    o_ref[...] = _residual_ln(h_ref[...], y, m_ref, G1, lng_ref, lnb_ref, alpha)


def _delta_out(o_f, o_b, proj, norm_g, w_out, h, mod, lng, lnb, tm, n_lat, batch, alpha):
    t, d = h.shape
    cw = C_HEADS * HEAD_DIM
    row = lambda w: pl.BlockSpec((tm, w), lambda i: (i, 0))
    return pl.pallas_call(
        functools.partial(_delta_out_kernel, alpha=alpha),
        grid=(t // tm,),
        in_specs=[
            row(cw), row(cw),
            pl.BlockSpec((tm, cw), lambda i: (i, 3)),
            pl.BlockSpec((1, HEAD_DIM), lambda i: (0, 0)),
            pl.BlockSpec((cw, d), lambda i: (0, 0)),
            row(d),
            _mod_spec(d, n_lat // tm, batch),
            pl.BlockSpec((1, d), lambda i: (0, 0)),
            pl.BlockSpec((1, d), lambda i: (0, 0)),
        ],
        out_specs=row(d),
        out_shape=jax.ShapeDtypeStruct((t, d), F32),
        scratch_shapes=[pltpu.VMEM((tm, cw), BF16)],
        compiler_params=_cparams(("parallel",)),
        name="delta_out_ln",
    )(o_f, o_b, proj, norm_g[None], w_out, h, mod, lng, lnb)


def _deltanet_layer(h, mod, w_in, conv_w, a_log, dt_bias, norm_g, w_out, lng, lnb, dims, alpha):
    batch, n_lat, n_ctx, tm = dims
    cw = C_HEADS * HEAD_DIM
    d = h.shape[1]
    n_gate = w_in.shape[1] - 4 * cw
    w = jnp.concatenate([w_in, jnp.zeros((d, LANES - n_gate), w_in.dtype)], 1).astype(BF16)
    proj, ab = _delta_proj(h, mod, w, tm, n_lat, batch)
    tmc = _pick_tile(256, n_lat, n_ctx)
    q, k, v, gb = _delta_prep(proj, ab, conv_w, a_log, dt_bias, tmc, n_lat, n_ctx, batch)
    o_f, o_b = _delta_scan(q, k, v, gb, n_lat, n_ctx, batch)
    return _delta_out(o_f, o_b, proj, norm_g, w_out.astype(BF16), h, mod, lng, lnb, tm, n_lat, batch, alpha)


PAGE = 16
PAGES_PER_BLOCK = 32


def _local_rows(tm):
    return TOP_K * tm + N_EXPERTS * PAGE


def _spare_blocks(tm):
    return -(-(2 * PAGES_PER_BLOCK + 1) // (_local_rows(tm) // PAGE))


def _router_kernel(h_ref, m_ref, wr_ref, bias_ref, xs_ref, rt_ref, npg_ref, *, nb):
    i = pl.program_id(0)

    @pl.when(i < nb)
    def _():
        _route_block(h_ref, m_ref, wr_ref, bias_ref, xs_ref, rt_ref, npg_ref)

    @pl.when(i >= nb)
    def _():
        xs_ref[...] = jnp.zeros(xs_ref.shape, xs_ref.dtype)


def _route_block(h_ref, m_ref, wr_ref, bias_ref, xs_ref, rt_ref, npg_ref):
    u = _modulate(h_ref[...], m_ref, SH2, SC2)
    tm = u.shape[0]
    u1, u2, _ = _split3(u)
    w1, w2, _ = _split3(wr_ref[...])
    nt = lambda a, b: lax.dot_general(a, b, (((1,), (1,)), ((), ())), preferred_element_type=F32)
    logits = nt(w1, u1) + nt(w1, u2) + nt(w2, u1)
    scores = jax.nn.sigmoid(logits)
    sel = scores + bias_ref[:, 0:1]
    per = N_EXPERTS // N_GROUPS
    best = None
    gidx = jnp.zeros((1, tm), jnp.int32)
    for g in range(N_GROUPS):
        r = [sel[g * per + j:g * per + j + 1, :] for j in range(per)]
        top2 = None
        for a in range(per):
            for b in range(a + 1, per):
                pair = r[a] + r[b]
                top2 = pair if top2 is None else jnp.maximum(top2, pair)
        if best is None:
            best = top2
        else:
            better = top2 > best
            gidx = jnp.where(better, g, gidx)
            best = jnp.where(better, top2, best)
    eio = lax.broadcasted_iota(jnp.int32, (N_EXPERTS, tm), 0)
    masked = jnp.where(eio // per == gidx, sel, -jnp.inf)
    m1 = jnp.max(masked, 0, keepdims=True)
    i1 = jnp.min(jnp.where(masked == m1, eio, N_EXPERTS), 0, keepdims=True)
    masked2 = jnp.where(eio == i1, -jnp.inf, masked)
    m2 = jnp.max(masked2, 0, keepdims=True)
    i2 = jnp.min(jnp.where(masked2 == m2, eio, N_EXPERTS), 0, keepdims=True)
    oh1 = eio == i1
    oh2 = eio == i2
    s1 = jnp.sum(jnp.where(oh1, scores, 0.0), 0, keepdims=True)
    s2 = jnp.sum(jnp.where(oh2, scores, 0.0), 0, keepdims=True)
    tot = s1 + s2
    oh = jnp.where(oh1, 1.0, jnp.where(oh2, 1.0, 0.0))
    ti = lax.broadcasted_iota(jnp.int32, (tm, tm), 0)
    tj = lax.broadcasted_iota(jnp.int32, (tm, tm), 1)
    before = jnp.where(ti < tj, 1.0, 0.0).astype(BF16)
    rank = jnp.dot(oh.astype(BF16), before, preferred_element_type=F32)
    cnt = jnp.sum(oh, 1, keepdims=True)
    npg = jnp.floor((cnt + (PAGE - 1)) * (1.0 / PAGE))
    ei = lax.broadcasted_iota(jnp.int32, (N_EXPERTS, N_EXPERTS), 0)
    ej = lax.broadcasted_iota(jnp.int32, (N_EXPERTS, N_EXPERTS), 1)
    lower = jnp.where(ej < ei, 1.0, 0.0).astype(BF16)
    npg_b = jnp.broadcast_to(npg, (N_EXPERTS, LANES))
    start_pg = jnp.dot(lower, npg_b.astype(BF16), preferred_element_type=F32)
    pos = start_pg[:, 0:1] * PAGE + rank
    p1 = jnp.sum(jnp.where(oh1, pos, 0.0), 0, keepdims=True)
    p2 = jnp.sum(jnp.where(oh2, pos, 0.0), 0, keepdims=True)
    rows = xs_ref.shape[0]
    rio = lax.broadcasted_iota(jnp.int32, (rows, tm), 0).astype(F32)
    perm = jnp.where(rio == p1, 1.0, jnp.where(rio == p2, 1.0, 0.0)).astype(BF16)
    xs_ref[...] = jnp.dot(perm, u.astype(BF16), preferred_element_type=F32).astype(BF16)
    rt_ref[...] = jnp.concatenate([s1 / tot, s2 / tot, p1, p2, jnp.zeros((SUBLANES - 4, tm), F32)], 0)
    npg_ref[...] = npg_b


def _router(h, mod, w_router, router_bias, tm, n_lat, batch):
    t, d = h.shape
    nb = t // tm
    rows = _local_rows(tm)
    wr = w_router.T.astype(F32)
    bias = jnp.broadcast_to(router_bias.astype(F32)[:, None], (N_EXPERTS, LANES))
    last = nb - 1
    ngrid = nb + _spare_blocks(tm)
    return pl.pallas_call(
        functools.partial(_router_kernel, nb=nb),
        grid=(ngrid,),
        in_specs=[
            pl.BlockSpec((tm, d), lambda i: (jnp.minimum(i, last), 0)),
            _mod_spec(d, n_lat // tm, batch),
            pl.BlockSpec((N_EXPERTS, d), lambda i: (0, 0)),
            pl.BlockSpec((N_EXPERTS, LANES), lambda i: (0, 0)),
        ],
        out_specs=[
            pl.BlockSpec((rows, d), lambda i: (i, 0)),
            pl.BlockSpec((SUBLANES, tm), lambda i: (0, jnp.minimum(i, last))),
            pl.BlockSpec((N_EXPERTS, LANES), lambda i: (jnp.minimum(i, last), 0)),
        ],
        out_shape=[
            jax.ShapeDtypeStruct((ngrid * rows, d), BF16),
            jax.ShapeDtypeStruct((SUBLANES, t), F32),
            jax.ShapeDtypeStruct((nb * N_EXPERTS, LANES), F32),
        ],
        compiler_params=_cparams(("arbitrary",)),
        name="moe_router",
    )(h, mod, wr, bias)


def _expert_kernel(pg_ref, be_ref, nu_ref, xs_in_hbm, wg_ref, wu_ref, wd_ref, xs_hbm,
                   xbuf, ybuf, in_sem, out_sem, *, nblk, spare_page0):
    del xs_in_hbm
    ys_hbm = xs_hbm
    i = pl.program_id(0)
    n_used = nu_ref[0]

    zero_page = spare_page0 + 2 * PAGES_PER_BLOCK

    def in_copy(blk, buf, j):
        page = pg_ref[blk * PAGES_PER_BLOCK + j]
        page = jnp.where(page >= 0, page, zero_page)
        return pltpu.make_async_copy(xs_hbm.at[pl.ds(pl.multiple_of(page * PAGE, PAGE), PAGE)],
                                     xbuf.at[buf, pl.ds(j * PAGE, PAGE)], in_sem.at[buf])

    def out_copy(blk, buf, j):
        page = pg_ref[blk * PAGES_PER_BLOCK + j]
        page = jnp.where(page >= 0, page, spare_page0 + buf * PAGES_PER_BLOCK + j)
        return pltpu.make_async_copy(ybuf.at[buf, pl.ds(j * PAGE, PAGE)],
                                     ys_hbm.at[pl.ds(pl.multiple_of(page * PAGE, PAGE), PAGE)], out_sem.at[buf])

    def fetch(blk, buf):
        for j in range(PAGES_PER_BLOCK):
            in_copy(blk, buf, j).start()

    def wait_out(blk, buf):
        for j in range(PAGES_PER_BLOCK):
            out_copy(blk, buf, j).wait()

    buf = i % 2

    @pl.when(jnp.logical_and(i == 0, n_used > 0))
    def _():
        fetch(0, 0)

    @pl.when(i + 1 < n_used)
    def _():
        fetch(i + 1, 1 - buf)

    @pl.when(i < n_used)
    def _():
        for j in range(PAGES_PER_BLOCK):
            in_copy(i, buf, j).wait()

        @pl.when(i >= 2)
        def _():
            wait_out(i - 2, buf)

        x = xbuf[buf]
        hid = _silu(_bdot(x, wg_ref[...])) * _bdot(x, wu_ref[...])
        ybuf[buf] = _bdot(hid, wd_ref[...]).astype(ybuf.dtype)
        for j in range(PAGES_PER_BLOCK):
            out_copy(i, buf, j).start()

    @pl.when(i == nblk - 1)
    def _():
        for back in (2, 1):
            @pl.when(n_used - back >= 0)
            def _():
                wait_out(n_used - back, (n_used - back) % 2)


def _experts(xs, slot_page, block_e, n_used, e_gate, e_up, e_down, layer, spare_page0):
    n_rows, d = xs.shape
    assert n_rows // PAGE - spare_page0 > 2 * PAGES_PER_BLOCK
    de = e_gate.shape[-1]
    nblk = block_e.shape[0]
    rows = PAGES_PER_BLOCK * PAGE
    any_spec = pl.BlockSpec(memory_space=pl.ANY)
    wspec = lambda a, b: pl.BlockSpec((None, None, a, b), lambda i, pg, be, nu: (layer, be[i], 0, 0))
    grid_spec = pltpu.PrefetchScalarGridSpec(
        num_scalar_prefetch=3,
        grid=(nblk,),
        in_specs=[any_spec, wspec(d, de), wspec(d, de), wspec(de, d)],
        out_specs=any_spec,
        scratch_shapes=[
            pltpu.VMEM((2, rows, d), BF16),
            pltpu.VMEM((2, rows, d), BF16),
            pltpu.SemaphoreType.DMA((2,)),
            pltpu.SemaphoreType.DMA((2,)),
        ],
    )
    return pl.pallas_call(
        functools.partial(_expert_kernel, nblk=nblk, spare_page0=spare_page0),
        grid_spec=grid_spec,
        out_shape=jax.ShapeDtypeStruct((n_rows, d), BF16),
        input_output_aliases={3: 0},
        compiler_params=_cparams(("arbitrary",)),
        name="moe_experts",
    )(slot_page, block_e, n_used, xs, e_gate, e_up, e_down)


def _moe_combine_kernel(ys_ref, rt_ref, h_ref, m_ref, lng_ref, lnb_ref, o_ref, *, alpha):
    tm = h_ref.shape[0]
    rows = ys_ref.shape[0]
    ti = lax.broadcasted_iota(jnp.int32, (tm, tm), 0)
    tj = lax.broadcasted_iota(jnp.int32, (tm, tm), 1)
    eye = jnp.where(ti == tj, 1.0, 0.0).astype(BF16)
    cols = sum(lax.dot_general(eye, piece, (((1,), (1,)), ((), ())), preferred_element_type=F32)
               for piece in _split3(rt_ref[...]))
    lane = lax.broadcasted_iota(jnp.int32, (tm, rows), 1).astype(F32)
    ys = ys_ref[...]
    y = None
    for k in range(TOP_K):
        sel = jnp.where(lane == cols[:, TOP_K + k:TOP_K + k + 1], 1.0, 0.0).astype(BF16)
        part = cols[:, k:k + 1] * jnp.dot(sel, ys, preferred_element_type=F32)
        y = part if y is None else y + part
    o_ref[...] = _residual_ln(h_ref[...], y, m_ref, G2, lng_ref, lnb_ref, alpha)


def _moe_combine(ys, rt, h, mod, lng, lnb, tm, n_lat, batch, alpha):
    t, d = h.shape
    rows = _local_rows(tm)
    row = pl.BlockSpec((tm, d), lambda i: (i, 0))
    return pl.pallas_call(
        functools.partial(_moe_combine_kernel, alpha=alpha),
        grid=(t // tm,),
        in_specs=[pl.BlockSpec((rows, d), lambda i: (i, 0)), pl.BlockSpec((SUBLANES, tm), lambda i: (0, i)),
                  row, _mod_spec(d, n_lat // tm, batch),
                  pl.BlockSpec((1, d), lambda i: (0, 0)), pl.BlockSpec((1, d), lambda i: (0, 0))],
        out_specs=row,
        out_shape=jax.ShapeDtypeStruct((t, d), F32),
        compiler_params=_cparams(("parallel",)),
        name="moe_combine_ln",
    )(ys, rt, h, mod, lng, lnb)


def _page_table(npg, pages_per_local):
    nb = npg.shape[0]
    ppb = PAGES_PER_BLOCK
    max_pages = nb * (pages_per_local - 1) + N_EXPERTS * (ppb - 1)
    nblk = -(-max_pages // ppb)
    lstart = jnp.cumsum(npg, 1) - npg
    pe = jnp.sum(npg, 0)
    pe_pad = (pe + ppb - 1) // ppb * ppb
    eend = jnp.cumsum(pe_pad)
    estart = eend - pe_pad
    run_pages = npg.T
    run_slot = (estart[:, None] + jnp.cumsum(run_pages, 1) - run_pages).reshape(-1)
    run_page = (jnp.arange(nb, dtype=jnp.int32)[None, :] * pages_per_local + lstart.T).reshape(-1)
    run_pages = run_pages.reshape(-1)
    s = jnp.arange(nblk * ppb, dtype=jnp.int32)[:, None]
    in_run = (run_slot[None, :] <= s) & (s < (run_slot + run_pages)[None, :])
    page = jnp.sum(jnp.where(in_run, (run_page - run_slot)[None, :] + s, 0), 1)
    slot_page = jnp.where(jnp.any(in_run, 1), page, -1).astype(jnp.int32)
    starts = jnp.arange(nblk, dtype=jnp.int32) * ppb
    block_e = jnp.minimum(jnp.sum((eend[None, :] <= starts[:, None]).astype(jnp.int32), 1), N_EXPERTS - 1)
    n_used = (eend[-1:] // ppb).astype(jnp.int32)
    return slot_page, block_e, n_used


def _moe_layer(h, mod, w_router, router_bias, e_gate, e_up, e_down, layer, lng, lnb, dims, alpha):
    batch, n_lat, n_ctx, tm = dims
    xs, rt, npg = _router(h, mod, w_router, router_bias, tm, n_lat, batch)
    npg = npg[:, 0].reshape(-1, N_EXPERTS).astype(jnp.int32)
    pages_per_local = _local_rows(tm) // PAGE
    slot_page, block_e, n_used = _page_table(npg, pages_per_local)
    ys = _experts(xs, slot_page, block_e, n_used, e_gate, e_up, e_down, layer, npg.shape[0] * pages_per_local)
    return _moe_combine(ys, rt, h, mod, lng, lnb, tm, n_lat, batch, alpha)


def kernel(x, c, ctx, c_ctx, w_ada, b_ada, ln_g, ln_b, a_wqkv, a_qg, a_kg, a_wo, b_win, b_vg, b_vb, b_ws, b_bs,
           b_wout, c_win, c_conv, c_alog, c_dtb, c_ng, c_wout, w_router, router_bias, e_gate, e_up, e_down):
    batch, n_lat, d = x.shape
    n_ctx = ctx.shape[1]
    depth = w_ada.shape[0]
    alpha = (2 * depth) ** 0.25
    assert batch + 1 <= SUBLANES
    tm = _pick_tile(512, n_lat, batch * n_ctx)
    dims = (batch, n_lat, n_ctx, tm)

    h = jnp.concatenate([x.reshape(batch * n_lat, d), ctx.reshape(batch * n_ctx, d)], 0).astype(F32)
    cond = jnp.concatenate([c, c_ctx[None], jnp.zeros((SUBLANES - batch - 1, d), c.dtype)], 0).astype(F32)
    ada = _ada_table(cond, w_ada, b_ada)
    mods = jnp.pad(ada.reshape(depth, SUBLANES, 6, d), ((0, 0), (0, 0), (0, SUBLANES - 6), (0, 0)))
    rope = _rope_tables(n_lat, tm)

    for i in range(depth):
        m, j = i % N_MIXERS, i // N_MIXERS
        mod = mods[i]
        lng1, lnb1 = ln_g[i, 0][None], ln_b[i, 0][None]
        lng2, lnb2 = ln_g[i, 1][None], ln_b[i, 1][None]
        if m == 0:
            h = _attention_layer(h, mod, a_wqkv[j], a_qg[j], a_kg[j], a_wo[j], lng1, lnb1, rope, dims, alpha)
        elif m == 1:
            h = _gmlp_layer(h, mod, b_win[j], b_vg[j], b_vb[j], b_ws[j], b_bs[j], b_wout[j], lng1, lnb1, dims, alpha)
        else:
            h = _deltanet_layer(h, mod, c_win[j], c_conv[j], c_alog[j], c_dtb[j], c_ng[j], c_wout[j],
                                lng1, lnb1, dims, alpha)
        h = _moe_layer(h, mod, w_router, router_bias, e_gate, e_up, e_down, i, lng2, lnb2, dims, alpha)
    return h[:batch * n_lat].reshape(batch, n_lat, d).astype(x.dtype)
```

```python
import functools
import math

import jax
import jax.numpy as jnp
from jax import lax
from jax.experimental import pallas as pl
from jax.experimental.pallas import tpu as pltpu

F32 = jnp.float32
BF16 = jnp.bfloat16

GRID_W = 64
A_HEADS = 8
A_KV_HEADS = 2
A_GROUP = A_HEADS // A_KV_HEADS
HEAD_DIM = 128
ROPE_THETA = 10000.0
B_CHUNK = 128
B_GROUPS = 8
C_HEADS = 8
C_CONV = 3
C_CHUNK = 64
N_EXPERTS = 16
N_GROUPS = 4
TOP_K = 2
EPS = 1e-6
N_MIXERS = 3

LANES = 128
SUBLANES = 8
VMEM_LIMIT_BYTES = 56 * 1024 * 1024
LOG2E = 1.4426950408889634


def _cparams(sem, vmem=VMEM_LIMIT_BYTES):
    return pltpu.CompilerParams(dimension_semantics=sem, vmem_limit_bytes=vmem)


def _pick_tile(cap, *sizes):
    t = cap
    while t > SUBLANES and any(s % t for s in sizes):
        t //= 2
    assert all(s % t == 0 for s in sizes), (cap, sizes)
    return t


def _layer_norm(x, g, b):
    mu = jnp.mean(x, -1, keepdims=True)
    xc = x - mu
    var = jnp.mean(xc * xc, -1, keepdims=True)
    return xc * lax.rsqrt(var + EPS) * g + b


def _silu(x):
    return x * jax.nn.sigmoid(x)


def _bdot(a, b):
    return jnp.dot(a.astype(BF16), b.astype(BF16), preferred_element_type=F32)


def _bdot_nt(a, b):
    return lax.dot_general(a.astype(BF16), b.astype(BF16), (((1,), (1,)), ((), ())), preferred_element_type=F32)


def _split3(x):
    x1 = x.astype(BF16)
    r = x - x1.astype(F32)
    x2 = r.astype(BF16)
    x3 = (r - x2.astype(F32)).astype(BF16)
    return x1, x2, x3


def _ada_kernel(c_ref, w_ref, b_ref, o_ref):
    s = _silu(c_ref[...])
    o_ref[...] = jnp.dot(s, w_ref[...], preferred_element_type=F32, precision=lax.Precision.HIGHEST) + b_ref[...]


def _ada_table(cond, w_ada, b_ada):
    depth, d, d6 = w_ada.shape
    tn = _pick_tile(1536, d6)
    return pl.pallas_call(
        _ada_kernel,
        grid=(depth, d6 // tn),
        in_specs=[
            pl.BlockSpec((SUBLANES, d), lambda i, j: (0, 0)),
            pl.BlockSpec((None, d, tn), lambda i, j: (i, 0, j)),
            pl.BlockSpec((None, 1, tn), lambda i, j: (i, 0, j)),
        ],
        out_specs=pl.BlockSpec((None, SUBLANES, tn), lambda i, j: (i, 0, j)),
        out_shape=jax.ShapeDtypeStruct((depth, SUBLANES, d6), F32),
        compiler_params=_cparams(("arbitrary", "arbitrary")),
        name="ada_table",
    )(cond, w_ada, b_ada.reshape(depth, 1, d6))


SH1, SC1, G1, SH2, SC2, G2 = range(6)


def _mod_spec(d, blocks_per_batch, batch):
    return pl.BlockSpec((None, SUBLANES, d), lambda i, *_: (jnp.minimum(i // blocks_per_batch, batch), 0, 0))


def _modulate(h, m_ref, sh, sc):
    return h * (1.0 + m_ref[sc:sc + 1, :]) + m_ref[sh:sh + 1, :]


def _residual_ln(h, y, m_ref, gate, lng_ref, lnb_ref, alpha):
    return _layer_norm(alpha * h + m_ref[gate:gate + 1, :] * y, lng_ref[...], lnb_ref[...])


def _attn_proj_kernel(h_ref, m_ref, w_ref, qg_ref, kg_ref, cos_ref, sin_ref, q_ref, k_ref, v_ref):
    u = _modulate(h_ref[...], m_ref, SH1, SC1)
    qkv = _bdot(u, w_ref[...])
    cos = cos_ref[...]
    sin = sin_ref[...]
    nq = A_HEADS * HEAD_DIM
    nk = A_KV_HEADS * HEAD_DIM

    def norm_rope(xh, g):
        xn = xh * lax.rsqrt(jnp.mean(xh * xh, -1, keepdims=True) + EPS) * g
        return xn * cos + pltpu.roll(xn, HEAD_DIM // 2, 1) * sin

    for hd in range(A_HEADS):
        sl = slice(hd * HEAD_DIM, (hd + 1) * HEAD_DIM)
        q_ref[:, sl] = norm_rope(qkv[:, sl], qg_ref[...]).astype(BF16)
    for hd in range(A_KV_HEADS):
        sl = slice(hd * HEAD_DIM, (hd + 1) * HEAD_DIM)
        k_ref[:, sl] = norm_rope(qkv[:, nq + hd * HEAD_DIM:nq + (hd + 1) * HEAD_DIM], kg_ref[...]).astype(BF16)
    v_ref[...] = qkv[:, nq + nk:].astype(BF16)


def _attn_proj(h, mod, w_qkv, qg, kg, cos_t, sin_t, tm, n_lat, batch):
    t, d = h.shape
    nq = A_HEADS * HEAD_DIM
    nk = A_KV_HEADS * HEAD_DIM
    bpb = n_lat // tm

    def rope_map(i):
        return (jnp.where(i < batch * bpb, i % bpb, bpb), 0)

    return pl.pallas_call(
        _attn_proj_kernel,
        grid=(t // tm,),
        in_specs=[
            pl.BlockSpec((tm, d), lambda i: (i, 0)),
            _mod_spec(d, bpb, batch),
            pl.BlockSpec((d, nq + 2 * nk), lambda i: (0, 0)),
            pl.BlockSpec((1, HEAD_DIM), lambda i: (0, 0)),
            pl.BlockSpec((1, HEAD_DIM), lambda i: (0, 0)),
            pl.BlockSpec((tm, HEAD_DIM), rope_map),
            pl.BlockSpec((tm, HEAD_DIM), rope_map),
        ],
        out_specs=[
            pl.BlockSpec((tm, nq), lambda i: (i, 0)),
            pl.BlockSpec((tm, nk), lambda i: (i, 0)),
            pl.BlockSpec((tm, nk), lambda i: (i, 0)),
        ],
        out_shape=[
            jax.ShapeDtypeStruct((t, nq), BF16),
            jax.ShapeDtypeStruct((t, nk), BF16),
            jax.ShapeDtypeStruct((t, nk), BF16),
        ],
        compiler_params=_cparams(("parallel",)),
        name="attn_proj",
    )(h, mod, w_qkv, qg, kg, cos_t, sin_t)


def _flash_kernel(q_ref, k_ref, v_ref, o_ref, m_ref, l_ref, acc_ref, *, nchunk, tk):
    m_ref[...] = jnp.full(m_ref.shape, -jnp.inf, F32)
    l_ref[...] = jnp.zeros(l_ref.shape, F32)
    acc_ref[...] = jnp.zeros(acc_ref.shape, F32)
    nslab = tk // LANES

    def body(j, carry):
        off = pl.multiple_of(j * tk, tk)
        kj = k_ref[pl.ds(off, tk), :]
        vj = v_ref[pl.ds(off, tk), :]

        def scores(g):
            qg = q_ref[:, g * HEAD_DIM:(g + 1) * HEAD_DIM]
            return lax.dot_general(qg, kj, (((1,), (1,)), ((), ())), preferred_element_type=F32)

        sc_next = scores(0)
        for g in range(A_GROUP):
            sc = sc_next
            if g + 1 < A_GROUP:
                sc_next = scores(g + 1)
            slabs = [sc[:, c * LANES:(c + 1) * LANES] for c in range(nslab)]
            mc = slabs[0]
            for sl in slabs[1:]:
                mc = jnp.maximum(mc, sl)
            m_old = m_ref[g]
            m_new = jnp.maximum(m_old, jnp.max(mc, -1, keepdims=True))
            alpha = jnp.exp2(m_old - m_new)
            ps = [jnp.exp2(sl - m_new) for sl in slabs]
            lsum = ps[0]
            for x in ps[1:]:
                lsum = lsum + x
            p = jnp.concatenate([x.astype(BF16) for x in ps], axis=1)
            l_ref[g] = alpha * l_ref[g] + lsum
            acc_ref[g] = alpha * acc_ref[g] + jnp.dot(p, vj, preferred_element_type=F32)
            m_ref[g] = m_new
        return carry

    lax.fori_loop(0, nchunk, body, 0, unroll=next(u for u in (3, 2, 1) if nchunk % u == 0))
    for g in range(A_GROUP):
        out = acc_ref[g] / jnp.sum(l_ref[g], -1, keepdims=True)
        o_ref[:, g * HEAD_DIM:(g + 1) * HEAD_DIM] = out.astype(o_ref.dtype)


def _key_tile(n):
    for tk in (1408, 768, 512, 384, 256, 128):
        if n % tk == 0:
            return tk
    raise ValueError(n)


def _flash(q, k, v, *, batch, q_row0, q_len, tq):
    gw = A_GROUP * HEAD_DIM
    nqb = q_len // tq
    n_keys = k.shape[1]
    tk = _key_tile(n_keys)
    kv_spec = pl.BlockSpec((None, n_keys, HEAD_DIM), lambda b, kv, i: (b, 0, kv))
    return pl.pallas_call(
        functools.partial(_flash_kernel, nchunk=n_keys // tk, tk=tk),
        grid=(batch, A_KV_HEADS, nqb),
        in_specs=[pl.BlockSpec((tq, gw), lambda b, kv, i: (q_row0 // tq + b * nqb + i, kv)), kv_spec, kv_spec],
        out_specs=pl.BlockSpec((tq, gw), lambda b, kv, i: (b * nqb + i, kv)),
        out_shape=jax.ShapeDtypeStruct((batch * q_len, A_HEADS * HEAD_DIM), BF16),
        scratch_shapes=[pltpu.VMEM((A_GROUP, tq, LANES), F32)] * 2 + [pltpu.VMEM((A_GROUP, tq, HEAD_DIM), F32)],
        compiler_params=_cparams(("parallel", "parallel", "arbitrary")),
        name="flash_attn",
    )(q, k, v)


def _out_proj_kernel(a_ref, w_ref, h_ref, m_ref, lng_ref, lnb_ref, o_ref, *, alpha):
    y = jnp.dot(a_ref[...], w_ref[...], preferred_element_type=F32)
    o_ref[...] = _residual_ln(h_ref[...], y, m_ref, G1, lng_ref, lnb_ref, alpha)


def _out_proj(a, w, h, mod, lng, lnb, tm, n_lat, batch, alpha):
    t, d = h.shape
    ka = a.shape[1]
    return pl.pallas_call(
        functools.partial(_out_proj_kernel, alpha=alpha),
        grid=(t // tm,),
        in_specs=[
            pl.BlockSpec((tm, ka), lambda i: (i, 0)),
            pl.BlockSpec((ka, d), lambda i: (0, 0)),
            pl.BlockSpec((tm, d), lambda i: (i, 0)),
            _mod_spec(d, n_lat // tm, batch),
            pl.BlockSpec((1, d), lambda i: (0, 0)),
            pl.BlockSpec((1, d), lambda i: (0, 0)),
        ],
        out_specs=pl.BlockSpec((tm, d), lambda i: (i, 0)),
        out_shape=jax.ShapeDtypeStruct((t, d), F32),
        compiler_params=_cparams(("parallel",)),
        name="out_proj_ln",
    )(a, w, h, mod, lng, lnb)


def _rope_tables(n, tm):
    rows = n // GRID_W
    half = HEAD_DIM // 2
    inv = 1.0 / (ROPE_THETA ** (jnp.arange(0, half, 2, dtype=F32) / half))
    r = jnp.broadcast_to(jnp.arange(rows, dtype=F32)[:, None], (rows, GRID_W)).reshape(-1)
    col = jnp.broadcast_to(jnp.arange(GRID_W, dtype=F32)[None, :], (rows, GRID_W)).reshape(-1)
    ang = jnp.concatenate([r[:, None] * inv, col[:, None] * inv], -1)
    cos, sin = jnp.cos(ang), jnp.sin(ang)
    cos_t = jnp.concatenate([jnp.concatenate([cos, cos], -1), jnp.ones((tm, HEAD_DIM), F32)], 0)
    sin_t = jnp.concatenate([jnp.concatenate([-sin, sin], -1), jnp.zeros((tm, HEAD_DIM), F32)], 0)
    return cos_t, sin_t


def _deinterleave_heads(w, n_heads):
    lead = w.shape[:-1]
    w = w.reshape(*lead, n_heads, HEAD_DIM // 2, 2)
    return jnp.swapaxes(w, -1, -2).reshape(*lead, n_heads * HEAD_DIM)


def _attention_layer(h, mod, w_qkv, q_g, k_g, w_o, lng, lnb, rope, dims, alpha):
    batch, n_lat, n_ctx, tm = dims
    nq = A_HEADS * HEAD_DIM
    nk = A_KV_HEADS * HEAD_DIM
    w = jnp.concatenate([
        _deinterleave_heads(w_qkv[:, :nq], A_HEADS),
        _deinterleave_heads(w_qkv[:, nq:nq + nk], A_KV_HEADS),
        w_qkv[:, nq + nk:]], -1).astype(BF16)
    qg = _deinterleave_heads(q_g[None, :], 1) * (HEAD_DIM ** -0.5 * LOG2E)
    kg = _deinterleave_heads(k_g[None, :], 1)
    q, k, v = _attn_proj(h, mod, w, qg, kg, rope[0], rope[1], tm, n_lat, batch)
    lat0 = batch * n_lat
    kvw = A_KV_HEADS * HEAD_DIM
    k_ctx, v_ctx = k[lat0:].reshape(batch, n_ctx, kvw), v[lat0:].reshape(batch, n_ctx, kvw)
    k_all = jnp.concatenate([k_ctx, k[:lat0].reshape(batch, n_lat, kvw)], 1)
    v_all = jnp.concatenate([v_ctx, v[:lat0].reshape(batch, n_lat, kvw)], 1)
    o_lat = _flash(q, k_all, v_all, batch=batch, q_row0=0, q_len=n_lat, tq=_pick_tile(512, n_lat))
    o_ctx = _flash(q, k_ctx, v_ctx, batch=batch, q_row0=lat0, q_len=n_ctx, tq=_pick_tile(256, n_ctx))
    o = jnp.concatenate([o_lat, o_ctx], 0)
    return _out_proj(o, w_o.astype(BF16), h, mod, lng, lnb, tm, n_lat, batch, alpha)


def _gmlp_kernel(h_ref, m_ref, win_ref, vg_ref, vb_ref, ws_ref, bs_ref, wout_ref, lng_ref, lnb_ref, o_ref,
                 gated_ref, *, alpha, width):
    h = h_ref[...]
    u = _modulate(h, m_ref, SH1, SC1)
    z = jax.nn.gelu(_bdot(u, win_ref[...]))
    uu = z[:, :width]
    v = _layer_norm(z[:, width:], vg_ref[...], vb_ref[...]).astype(BF16)
    tm = h.shape[0]
    gw = width // B_GROUPS
    for c in range(tm // B_CHUNK):
        rs = slice(c * B_CHUNK, (c + 1) * B_CHUNK)
        for g in range(B_GROUPS):
            cs = slice(g * gw, (g + 1) * gw)
            mixed = jnp.dot(ws_ref[g], v[rs, cs], preferred_element_type=F32) + bs_ref[g]
            gated_ref[rs, cs] = (uu[rs, cs] * mixed).astype(BF16)
    y = jnp.dot(gated_ref[...], wout_ref[...], preferred_element_type=F32)
    o_ref[...] = _residual_ln(h, y, m_ref, G1, lng_ref, lnb_ref, alpha)


def _gmlp_layer(h, mod, w_in, v_g, v_b, w_s, b_s, w_out, lng, lnb, dims, alpha):
    batch, n_lat, n_ctx, tm = dims
    t, d = h.shape
    width = w_out.shape[0]
    gw = width // B_GROUPS
    assert tm % B_CHUNK == 0 and n_ctx % B_CHUNK == 0
    bs = jnp.broadcast_to(b_s[:, :, None], (B_GROUPS, B_CHUNK, gw)).astype(F32)
    const = lambda *shape: pl.BlockSpec(shape, lambda i: (0,) * len(shape))
    return pl.pallas_call(
        functools.partial(_gmlp_kernel, alpha=alpha, width=width),
        grid=(t // tm,),
        in_specs=[
            pl.BlockSpec((tm, d), lambda i: (i, 0)),
            _mod_spec(d, n_lat // tm, batch),
            const(d, 2 * width), const(1, width), const(1, width),
            const(B_GROUPS, B_CHUNK, B_CHUNK), const(B_GROUPS, B_CHUNK, gw),
            const(width, d), const(1, d), const(1, d),
        ],
        out_specs=pl.BlockSpec((tm, d), lambda i: (i, 0)),
        out_shape=jax.ShapeDtypeStruct((t, d), F32),
        scratch_shapes=[pltpu.VMEM((tm, width), BF16)],
        compiler_params=_cparams(("parallel",)),
        name="gmlp_layer",
    )(h, mod, w_in.astype(BF16), v_g[None], v_b[None], w_s.astype(BF16), bs, w_out.astype(BF16), lng, lnb)


def _delta_proj_kernel(h_ref, m_ref, w_ref, wg_ref, o_ref, og_ref, *, tn):
    u = _modulate(h_ref[...], m_ref, SH1, SC1).astype(BF16)
    for j in range(w_ref.shape[1] // tn):
        o_ref[:, j * tn:(j + 1) * tn] = jnp.dot(u, w_ref[:, j * tn:(j + 1) * tn], preferred_element_type=F32)
    og_ref[...] = jnp.dot(u, wg_ref[...], preferred_element_type=F32)


def _delta_proj(h, mod, w, tm, n_lat, batch):
    t, d = h.shape
    ncol = w.shape[1] - LANES
    return pl.pallas_call(
        functools.partial(_delta_proj_kernel, tn=_pick_tile(1024, ncol)),
        grid=(t // tm,),
        in_specs=[
            pl.BlockSpec((tm, d), lambda i: (i, 0)),
            _mod_spec(d, n_lat // tm, batch),
            pl.BlockSpec((d, ncol), lambda i: (0, 0)),
            pl.BlockSpec((d, LANES), lambda i: (0, 0)),
        ],
        out_specs=[pl.BlockSpec((tm, ncol), lambda i: (i, 0)), pl.BlockSpec((tm, LANES), lambda i: (i, 0))],
        out_shape=[jax.ShapeDtypeStruct((t, ncol), F32), jax.ShapeDtypeStruct((t, LANES), F32)],
        compiler_params=_cparams(("parallel",)),
        name="delta_proj",
    )(h, mod, w[:, :ncol], w[:, ncol:])


def _delta_prep_kernel(x_ref, xp_ref, xn_ref, ab_ref, cw_ref, ga_ref, dtb_ref, q_ref, k_ref, v_ref, gb_ref,
                       *, tm, n_lat_rows, n_lat, n_ctx, cw):
    i = pl.program_id(0)
    row0 = i * tm
    in_lat = row0 < n_lat_rows
    pos = jnp.where(in_lat, row0 % n_lat, (row0 - n_lat_rows) % n_ctx)
    seq = jnp.where(in_lat, n_lat, n_ctx)
    has_prev = (pos != 0).astype(F32)
    has_next = (pos + tm != seq).astype(F32)
    rows = lax.broadcasted_iota(jnp.int32, (tm, HEAD_DIM), 0)
    first = rows == 0
    last = rows == tm - 1
    heads = cw // HEAD_DIM
    for part, out_ref in enumerate((q_ref, k_ref, v_ref)):
        for hd in range(heads):
            sl = slice(part * cw + hd * HEAD_DIM, part * cw + (hd + 1) * HEAD_DIM)
            x = x_ref[:, sl]
            prev_row = xp_ref[SUBLANES - 1:SUBLANES, sl] * has_prev
            next_row = xn_ref[0:1, sl] * has_next
            x_prev = jnp.where(first, prev_row, pltpu.roll(x, 1, 0))
            x_next = jnp.where(last, next_row, pltpu.roll(x, tm - 1, 0))
            y = _silu(x_prev * cw_ref[0:1, sl] + x * cw_ref[1:2, sl] + x_next * cw_ref[2:3, sl])
            if part < 2:
                y = y * lax.rsqrt(jnp.sum(y * y, -1, keepdims=True) + EPS)
            if part == 0:
                y = y * (HEAD_DIM ** -0.5)
            out_ref[:, hd * HEAD_DIM:(hd + 1) * HEAD_DIM] = y.astype(out_ref.dtype)
    ab = ab_ref[...]
    xg = ab + dtb_ref[...]
    softplus = jnp.maximum(xg, 0.0) + jnp.log1p(jnp.exp(-jnp.abs(xg)))
    lane = lax.broadcasted_iota(jnp.int32, ab.shape, 1)
    gb_ref[...] = jnp.where(lane < 2 * C_HEADS, ga_ref[...] * softplus, jax.nn.sigmoid(ab))


def _delta_prep(proj, ab, conv_w, a_log, dt_bias, tm, n_lat, n_ctx, batch):
    t = proj.shape[0]
    cw = C_HEADS * HEAD_DIM
    r8 = tm // SUBLANES
    nb8 = t // SUBLANES
    ga = jnp.zeros((1, LANES), F32).at[0, :2 * C_HEADS].set(-jnp.exp(a_log.astype(F32)).reshape(-1))
    dtb = jnp.zeros((1, LANES), F32).at[0, :2 * C_HEADS].set(dt_bias.astype(F32).reshape(-1))
    cwp = jnp.zeros((SUBLANES, 3 * cw), F32).at[:C_CONV].set(conv_w)
    return pl.pallas_call(
        functools.partial(_delta_prep_kernel, tm=tm, n_lat_rows=batch * n_lat, n_lat=n_lat, n_ctx=n_ctx, cw=cw),
        grid=(t // tm,),
        in_specs=[
            pl.BlockSpec((tm, 3 * cw), lambda i: (i, 0)),
            pl.BlockSpec((SUBLANES, 3 * cw), lambda i: (jnp.maximum(i * r8 - 1, 0), 0)),
            pl.BlockSpec((SUBLANES, 3 * cw), lambda i: (jnp.minimum((i + 1) * r8, nb8 - 1), 0)),
            pl.BlockSpec((tm, LANES), lambda i: (i, 0)),
            pl.BlockSpec((SUBLANES, 3 * cw), lambda i: (0, 0)),
            pl.BlockSpec((1, LANES), lambda i: (0, 0)),
            pl.BlockSpec((1, LANES), lambda i: (0, 0)),
        ],
        out_specs=[pl.BlockSpec((tm, cw), lambda i: (i, 0))] * 3 + [pl.BlockSpec((tm, LANES), lambda i: (i, 0))],
        out_shape=[jax.ShapeDtypeStruct((t, cw), BF16)] * 3 + [jax.ShapeDtypeStruct((t, LANES), F32)],
        compiler_params=_cparams(("parallel",)),
        name="delta_prep",
    )(proj, proj, proj, ab, cwp, ga, dtb)


DELTA_BLOCK = 16


def _delta_scan_kernel(*refs):
    nst = 2
    in_refs = [refs[4 * st:4 * st + 4] for st in range(nst)]
    out_refs = refs[4 * nst:5 * nst]
    s_ref, wq_s, u_s, at_s, kd_s, gl_s = refs[5 * nst:]
    c = pl.program_id(1)

    @pl.when(c == 0)
    def _():
        for ref in (s_ref, wq_s, u_s, at_s, kd_s, gl_s):
            ref[...] = jnp.zeros(ref.shape, ref.dtype)

    L = C_CHUNK
    ri = lax.broadcasted_iota(jnp.int32, (L, L), 0)
    ci = lax.broadcasted_iota(jnp.int32, (L, L), 1)
    eye = (ri == ci).astype(F32)
    blk_mask = (ri // DELTA_BLOCK) == (ci // DELTA_BLOCK)
    dirs = []
    for st, (q_ref, k_ref, v_ref, g_ref) in enumerate(in_refs):
        direction = st % 2
        tri, strict = (ri >= ci, ri > ci) if direction == 0 else (ri <= ci, ri < ci)
        tri_b = tri.astype(BF16)
        gb = g_ref[...]
        g1, g2, g3 = _split3(gb)
        gc = (jnp.dot(tri_b, g1, preferred_element_type=F32) + jnp.dot(tri_b, g2, preferred_element_type=F32)
              + jnp.dot(tri_b, g3, preferred_element_type=F32))
        dirs.append((q_ref, k_ref, v_ref, gb, gc, gc.T, tri, strict, L - 1 if direction == 0 else 0))

    chains = [(st, hd) for st in range(nst) for hd in range(C_HEADS)]

    def gate_lane(d, hd):
        return (d % 2) * C_HEADS + hd
    nch = len(chains)

    def per_chain(fn):
        return [fn(i, *chains[i]) for i in range(nch)]

    def lanes(hd):
        return slice(hd * HEAD_DIM, (hd + 1) * HEAD_DIM)

    s_old = per_chain(lambda i, d, hd: s_ref[i])
    ws = per_chain(lambda i, d, hd: jnp.dot(wq_s[i], s_old[i].astype(BF16), preferred_element_type=F32))

    kh = per_chain(lambda i, d, hd: dirs[d][1][:, lanes(hd)])
    qh = per_chain(lambda i, d, hd: dirs[d][0][:, lanes(hd)])
    kq = per_chain(lambda i, d, hd: _bdot_nt(jnp.concatenate([kh[i], qh[i]], 0), kh[i]))
    g_col = per_chain(lambda i, d, hd: dirs[d][4][:, gate_lane(d, hd):gate_lane(d, hd) + 1])
    g_row = per_chain(lambda i, d, hd: dirs[d][5][gate_lane(d, hd):gate_lane(d, hd) + 1, :])
    beta = per_chain(lambda i, d, hd: dirs[d][3][:, 2 * C_HEADS + gate_lane(d, hd):2 * C_HEADS + gate_lane(d, hd) + 1])
    g_last = per_chain(lambda i, d, hd: g_col[i][dirs[d][8]:dirs[d][8] + 1, :])
    decay = per_chain(lambda i, d, hd: jnp.exp(jnp.where(dirs[d][6], g_col[i] - g_row[i], -jnp.inf)))
    a_mat = per_chain(lambda i, d, hd: jnp.where(dirs[d][7], kq[i][:L] * beta[i] * decay[i], 0.0))
    attn = per_chain(lambda i, d, hd: kq[i][L:] * decay[i])
    eg = per_chain(lambda i, d, hd: jnp.exp(g_col[i]))
    rhs = per_chain(lambda i, d, hd: jnp.concatenate(
        [dirs[d][2][:, lanes(hd)].astype(F32) * beta[i], kh[i].astype(F32) * (beta[i] * eg[i])], 1))
    q_dec = per_chain(lambda i, d, hd: qh[i].astype(F32) * eg[i])
    k_dec = per_chain(lambda i, d, hd: kh[i].astype(F32) * jnp.exp(g_last[i] - g_col[i]))

    v_new = per_chain(lambda i, d, hd: (u_s[i] - ws[i][:L]).astype(BF16))
    o = per_chain(lambda i, d, hd: ws[i][L:] + jnp.dot(at_s[i], v_new[i], preferred_element_type=F32))
    kv = per_chain(lambda i, d, hd: lax.dot_general(
        kd_s[i], v_new[i], (((0,), (0,)), ((), ())), preferred_element_type=F32))
    for i, (d, hd) in enumerate(chains):
        s_ref[i] = s_old[i] * gl_s[i][0:1, :] + kv[i]
        out_refs[d][:, lanes(hd)] = o[i].astype(out_refs[d].dtype)

    n = per_chain(lambda i, d, hd: jnp.where(blk_mask, a_mat[i], 0.0))
    off = per_chain(lambda i, d, hd: a_mat[i] - n[i])
    n2 = per_chain(lambda i, d, hd: _bdot(n[i], n[i]))
    n4 = per_chain(lambda i, d, hd: _bdot(n2[i], n2[i]))
    dv = per_chain(lambda i, d, hd: eye - n[i] + n2[i] - _bdot(n[i], n2[i]))
    n8 = per_chain(lambda i, d, hd: _bdot(n4[i], n4[i]))
    dv = per_chain(lambda i, d, hd: dv[i] + _bdot(dv[i], n4[i]))
    dv = per_chain(lambda i, d, hd: dv[i] + _bdot(dv[i], n8[i]))
    m = per_chain(lambda i, d, hd: _bdot(dv[i], off[i]))
    r1 = per_chain(lambda i, d, hd: _bdot(dv[i], rhs[i]))
    m2 = per_chain(lambda i, d, hd: _bdot(m[i], m[i]))
    x1 = per_chain(lambda i, d, hd: r1[i] - _bdot(m[i], r1[i]))
    uw = per_chain(lambda i, d, hd: x1[i] + _bdot(m2[i], x1[i]))

    for i in range(nch):
        wq_s[i] = jnp.concatenate([uw[i][:, HEAD_DIM:], q_dec[i]], 0).astype(BF16)
        u_s[i] = uw[i][:, :HEAD_DIM]
        at_s[i] = attn[i].astype(BF16)
        kd_s[i] = k_dec[i].astype(BF16)
        gl_s[i] = jnp.broadcast_to(jnp.exp(g_last[i]), gl_s.shape[1:])


def _delta_scan(q, k, v, gb, n_lat, n_ctx, batch):
    t, cw = q.shape
    nlc, ncc = n_lat // C_CHUNK, n_ctx // C_CHUNK
    lat_blocks = batch * nlc

    def fwd_map(b, c):
        return jnp.where(c < ncc, lat_blocks + b * ncc + c, b * nlc + (c - ncc))

    def bwd_map(b, c):
        return jnp.where(c < ncc, lat_blocks + b * ncc + (ncc - 1 - c), b * nlc + (nlc - 1 - (c - ncc)))

    nch = ncc + nlc
    fin = lambda w: pl.BlockSpec((C_CHUNK, w), lambda b, c: (fwd_map(b, jnp.minimum(c, nch - 1)), 0))
    bin_ = lambda w: pl.BlockSpec((C_CHUNK, w), lambda b, c: (bwd_map(b, jnp.minimum(c, nch - 1)), 0))
    fout = pl.BlockSpec((C_CHUNK, cw), lambda b, c: (fwd_map(b, jnp.maximum(c - 1, 0)), 0))
    bout = pl.BlockSpec((C_CHUNK, cw), lambda b, c: (bwd_map(b, jnp.maximum(c - 1, 0)), 0))
    nchain = 2 * C_HEADS
    return pl.pallas_call(
        _delta_scan_kernel,
        grid=(batch, nch + 1),
        in_specs=[fin(cw), fin(cw), fin(cw), fin(LANES), bin_(cw), bin_(cw), bin_(cw), bin_(LANES)],
        out_specs=[fout, bout],
        out_shape=[jax.ShapeDtypeStruct((t, cw), BF16)] * 2,
        scratch_shapes=[
            pltpu.VMEM((nchain, HEAD_DIM, HEAD_DIM), F32),
            pltpu.VMEM((nchain, 2 * C_CHUNK, HEAD_DIM), BF16),
            pltpu.VMEM((nchain, C_CHUNK, HEAD_DIM), F32),
            pltpu.VMEM((nchain, C_CHUNK, C_CHUNK), BF16),
            pltpu.VMEM((nchain, C_CHUNK, HEAD_DIM), BF16),
            pltpu.VMEM((nchain, SUBLANES, LANES), F32),
        ],
        compiler_params=_cparams(("parallel", "arbitrary")),
        name="delta_scan",
    )(q, k, v, gb, q, k, v, gb)


def _delta_out_kernel(of_ref, ob_ref, z_ref, ng_ref, w_ref, h_ref, m_ref, lng_ref, lnb_ref, o_ref, y_ref, *, alpha):
    for hd in range(C_HEADS):
        sl = slice(hd * HEAD_DIM, (hd + 1) * HEAD_DIM)
        o = of_ref[:, sl].astype(F32) + ob_ref[:, sl].astype(F32)
        on = o * lax.rsqrt(jnp.mean(o * o, -1, keepdims=True) + EPS) * ng_ref[...]
        y_ref[:, sl] = (on * _silu(z_ref[:, sl])).astype(BF16)
    y = jnp.dot(y_ref[...], w_ref[...], preferred_element_type=F32)
    o_ref[...] = _residual_ln(h_ref[...], y, m_ref, G1, lng_ref, lnb_ref, alpha)


def _delta_out(o_f, o_b, proj, norm_g, w_out, h, mod, lng, lnb, tm, n_lat, batch, alpha):
    t, d = h.shape
    cw = C_HEADS * HEAD_DIM
    row = lambda w: pl.BlockSpec((tm, w), lambda i: (i, 0))
    return pl.pallas_call(
        functools.partial(_delta_out_kernel, alpha=alpha),
        grid=(t // tm,),
        in_specs=[
            row(cw), row(cw),
            pl.BlockSpec((tm, cw), lambda i: (i, 3)),
            pl.BlockSpec((1, HEAD_DIM), lambda i: (0, 0)),
            pl.BlockSpec((cw, d), lambda i: (0, 0)),
            row(d),
            _mod_spec(d, n_lat // tm, batch),
            pl.BlockSpec((1, d), lambda i: (0, 0)),
            pl.BlockSpec((1, d), lambda i: (0, 0)),
        ],
        out_specs=row(d),
        out_shape=jax.ShapeDtypeStruct((t, d), F32),
        scratch_shapes=[pltpu.VMEM((tm, cw), BF16)],
        compiler_params=_cparams(("parallel",)),
        name="delta_out_ln",
    )(o_f, o_b, proj, norm_g[None], w_out, h, mod, lng, lnb)


def _deltanet_layer(h, mod, w_in, conv_w, a_log, dt_bias, norm_g, w_out, lng, lnb, dims, alpha):
    batch, n_lat, n_ctx, tm = dims
    cw = C_HEADS * HEAD_DIM
    d = h.shape[1]
    n_gate = w_in.shape[1] - 4 * cw
    w = jnp.concatenate([w_in, jnp.zeros((d, LANES - n_gate), w_in.dtype)], 1).astype(BF16)
    proj, ab = _delta_proj(h, mod, w, tm, n_lat, batch)
    tmc = _pick_tile(256, n_lat, n_ctx)
    q, k, v, gb = _delta_prep(proj, ab, conv_w, a_log, dt_bias, tmc, n_lat, n_ctx, batch)
    o_f, o_b = _delta_scan(q, k, v, gb, n_lat, n_ctx, batch)
    return _delta_out(o_f, o_b, proj, norm_g, w_out.astype(BF16), h, mod, lng, lnb, tm, n_lat, batch, alpha)


PAGE = 16
PAGES_PER_BLOCK = 32


def _local_rows(tm):
    return TOP_K * tm + N_EXPERTS * PAGE


def _spare_blocks(tm):
    return -(-(2 * PAGES_PER_BLOCK + 1) // (_local_rows(tm) // PAGE))


def _router_kernel(h_ref, m_ref, wr_ref, bias_ref, xs_ref, rt_ref, npg_ref, *, nb):
    i = pl.program_id(0)

    @pl.when(i < nb)
    def _():
        _route_block(h_ref, m_ref, wr_ref, bias_ref, xs_ref, rt_ref, npg_ref)

    @pl.when(i >= nb)
    def _():
        xs_ref[...] = jnp.zeros(xs_ref.shape, xs_ref.dtype)


def _route_block(h_ref, m_ref, wr_ref, bias_ref, xs_ref, rt_ref, npg_ref):
    u = _modulate(h_ref[...], m_ref, SH2, SC2)
    tm = u.shape[0]
    u1, u2, _ = _split3(u)
    w1, w2, _ = _split3(wr_ref[...])
    nt = lambda a, b: lax.dot_general(a, b, (((1,), (1,)), ((), ())), preferred_element_type=F32)
    logits = nt(w1, u1) + nt(w1, u2) + nt(w2, u1)
    scores = jax.nn.sigmoid(logits)
    sel = scores + bias_ref[:, 0:1]
    per = N_EXPERTS // N_GROUPS
    best = None
    gidx = jnp.zeros((1, tm), jnp.int32)
    for g in range(N_GROUPS):
        r = [sel[g * per + j:g * per + j + 1, :] for j in range(per)]
        top2 = None
        for a in range(per):
            for b in range(a + 1, per):
                pair = r[a] + r[b]
                top2 = pair if top2 is None else jnp.maximum(top2, pair)
        if best is None:
            best = top2
        else:
            better = top2 > best
            gidx = jnp.where(better, g, gidx)
            best = jnp.where(better, top2, best)
    eio = lax.broadcasted_iota(jnp.int32, (N_EXPERTS, tm), 0)
    masked = jnp.where(eio // per == gidx, sel, -jnp.inf)
    m1 = jnp.max(masked, 0, keepdims=True)
    i1 = jnp.min(jnp.where(masked == m1, eio, N_EXPERTS), 0, keepdims=True)
    masked2 = jnp.where(eio == i1, -jnp.inf, masked)
    m2 = jnp.max(masked2, 0, keepdims=True)
    i2 = jnp.min(jnp.where(masked2 == m2, eio, N_EXPERTS), 0, keepdims=True)
    oh1 = eio == i1
    oh2 = eio == i2
    s1 = jnp.sum(jnp.where(oh1, scores, 0.0), 0, keepdims=True)
    s2 = jnp.sum(jnp.where(oh2, scores, 0.0), 0, keepdims=True)
    tot = s1 + s2
    oh = jnp.where(oh1, 1.0, jnp.where(oh2, 1.0, 0.0))
    ti = lax.broadcasted_iota(jnp.int32, (tm, tm), 0)
    tj = lax.broadcasted_iota(jnp.int32, (tm, tm), 1)
    before = jnp.where(ti < tj, 1.0, 0.0).astype(BF16)
    rank = jnp.dot(oh.astype(BF16), before, preferred_element_type=F32)
    cnt = jnp.sum(oh, 1, keepdims=True)
    npg = jnp.floor((cnt + (PAGE - 1)) * (1.0 / PAGE))
    ei = lax.broadcasted_iota(jnp.int32, (N_EXPERTS, N_EXPERTS), 0)
    ej = lax.broadcasted_iota(jnp.int32, (N_EXPERTS, N_EXPERTS), 1)
    lower = jnp.where(ej < ei, 1.0, 0.0).astype(BF16)
    npg_b = jnp.broadcast_to(npg, (N_EXPERTS, LANES))
    start_pg = jnp.dot(lower, npg_b.astype(BF16), preferred_element_type=F32)
    pos = start_pg[:, 0:1] * PAGE + rank
    p1 = jnp.sum(jnp.where(oh1, pos, 0.0), 0, keepdims=True)
    p2 = jnp.sum(jnp.where(oh2, pos, 0.0), 0, keepdims=True)
    rows = xs_ref.shape[0]
    rc = 2 * LANES
    assert rows % rc == 0
    ub = u1
    for c in range(rows // rc):
        rio = (lax.broadcasted_iota(jnp.int32, (rc, tm), 0) + c * rc).astype(F32)
        perm = jnp.where(rio == p1, 1.0, jnp.where(rio == p2, 1.0, 0.0)).astype(BF16)
        xs_ref[c * rc:(c + 1) * rc, :] = jnp.dot(perm, ub, preferred_element_type=F32).astype(BF16)
    rt_ref[...] = jnp.concatenate([s1 / tot, s2 / tot, p1, p2, jnp.zeros((SUBLANES - 4, tm), F32)], 0)
    npg_ref[...] = npg_b


def _router(h, mod, w_router, router_bias, tm, n_lat, batch):
    t, d = h.shape
    nb = t // tm
    rows = _local_rows(tm)
    wr = w_router.T.astype(F32)
    bias = jnp.broadcast_to(router_bias.astype(F32)[:, None], (N_EXPERTS, LANES))
    last = nb - 1
    ngrid = nb + _spare_blocks(tm)
    return pl.pallas_call(
        functools.partial(_router_kernel, nb=nb),
        grid=(ngrid,),
        in_specs=[
            pl.BlockSpec((tm, d), lambda i: (jnp.minimum(i, last), 0)),
            _mod_spec(d, n_lat // tm, batch),
            pl.BlockSpec((N_EXPERTS, d), lambda i: (0, 0)),
            pl.BlockSpec((N_EXPERTS, LANES), lambda i: (0, 0)),
        ],
        out_specs=[
            pl.BlockSpec((rows, d), lambda i: (i, 0)),
            pl.BlockSpec((SUBLANES, tm), lambda i: (0, jnp.minimum(i, last))),
            pl.BlockSpec((N_EXPERTS, LANES), lambda i: (jnp.minimum(i, last), 0)),
        ],
        out_shape=[
            jax.ShapeDtypeStruct((ngrid * rows, d), BF16),
            jax.ShapeDtypeStruct((SUBLANES, t), F32),
            jax.ShapeDtypeStruct((nb * N_EXPERTS, LANES), F32),
        ],
        compiler_params=_cparams(("arbitrary",)),
        name="moe_router",
    )(h, mod, wr, bias)


def _expert_kernel(pg_ref, be_ref, nu_ref, xs_in_hbm, wg_ref, wu_ref, wd_ref, xs_hbm,
                   xbuf, ybuf, wg_s, wu_s, wd_s, in_sem, out_sem, *, nblk, spare_page0):
    del xs_in_hbm
    ys_hbm = xs_hbm
    i = pl.program_id(0)
    n_used = nu_ref[0]

    zero_page = spare_page0 + 2 * PAGES_PER_BLOCK

    def in_copy(blk, buf, j):
        page = pg_ref[blk * PAGES_PER_BLOCK + j]
        page = jnp.where(page >= 0, page, zero_page)
        return pltpu.make_async_copy(xs_hbm.at[pl.ds(pl.multiple_of(page * PAGE, PAGE), PAGE)],
                                     xbuf.at[buf, pl.ds(j * PAGE, PAGE)], in_sem.at[buf])

    def out_copy(blk, buf, j):
        page = pg_ref[blk * PAGES_PER_BLOCK + j]
        page = jnp.where(page >= 0, page, spare_page0 + buf * PAGES_PER_BLOCK + j)
        return pltpu.make_async_copy(ybuf.at[buf, pl.ds(j * PAGE, PAGE)],
                                     ys_hbm.at[pl.ds(pl.multiple_of(page * PAGE, PAGE), PAGE)], out_sem.at[buf])

    def fetch(blk, buf):
        for j in range(PAGES_PER_BLOCK):
            in_copy(blk, buf, j).start()

    def wait_out(blk, buf):
        for j in range(PAGES_PER_BLOCK):
            out_copy(blk, buf, j).wait()

    buf = i % 2

    @pl.when(jnp.logical_and(i == 0, n_used > 0))
    def _():
        fetch(0, 0)

    @pl.when(i + 1 < n_used)
    def _():
        fetch(i + 1, 1 - buf)

    @pl.when(i < n_used)
    def _():
        for j in range(PAGES_PER_BLOCK):
            in_copy(i, buf, j).wait()

        @pl.when(i >= 2)
        def _():
            wait_out(i - 2, buf)

        @pl.when(jnp.logical_or(i == 0, be_ref[i] != be_ref[jnp.maximum(i - 1, 0)]))
        def _():
            wg_s[...] = wg_ref[...].astype(BF16)
            wu_s[...] = wu_ref[...].astype(BF16)
            wd_s[...] = wd_ref[...].astype(BF16)

        half = xbuf.shape[1] // 2
        xs_ = [xbuf[buf, pl.ds(r * half, half), :] for r in range(2)]
        gate = [jnp.dot(x, wg_s[...], preferred_element_type=F32) for x in xs_]
        up = [jnp.dot(x, wu_s[...], preferred_element_type=F32) for x in xs_]
        for r in range(2):
            hid = (_silu(gate[r]) * up[r]).astype(BF16)
            ybuf[buf, pl.ds(r * half, half), :] = jnp.dot(
                hid, wd_s[...], preferred_element_type=F32).astype(ybuf.dtype)
        for j in range(PAGES_PER_BLOCK):
            out_copy(i, buf, j).start()

    @pl.when(i == nblk - 1)
    def _():
        for back in (2, 1):
            @pl.when(n_used - back >= 0)
            def _():
                wait_out(n_used - back, (n_used - back) % 2)


def _experts(xs, slot_page, block_e, n_used, e_gate, e_up, e_down, layer, spare_page0):
    n_rows, d = xs.shape
    assert n_rows // PAGE - spare_page0 > 2 * PAGES_PER_BLOCK
    de = e_gate.shape[-1]
    nblk = block_e.shape[0]
    rows = PAGES_PER_BLOCK * PAGE
    any_spec = pl.BlockSpec(memory_space=pl.ANY)
    wspec = lambda a, b: pl.BlockSpec((None, None, a, b), lambda i, pg, be, nu: (layer, be[i], 0, 0))
    grid_spec = pltpu.PrefetchScalarGridSpec(
        num_scalar_prefetch=3,
        grid=(nblk,),
        in_specs=[any_spec, wspec(d, de), wspec(d, de), wspec(de, d)],
        out_specs=any_spec,
        scratch_shapes=[
            pltpu.VMEM((2, rows, d), BF16),
            pltpu.VMEM((2, rows, d), BF16),
            pltpu.VMEM((d, de), BF16),
            pltpu.VMEM((d, de), BF16),
            pltpu.VMEM((de, d), BF16),
            pltpu.SemaphoreType.DMA((2,)),
            pltpu.SemaphoreType.DMA((2,)),
        ],
    )
    return pl.pallas_call(
        functools.partial(_expert_kernel, nblk=nblk, spare_page0=spare_page0),
        grid_spec=grid_spec,
        out_shape=jax.ShapeDtypeStruct((n_rows, d), BF16),
        input_output_aliases={3: 0},
        compiler_params=_cparams(("arbitrary",)),
        name="moe_experts",
    )(slot_page, block_e, n_used, xs, e_gate, e_up, e_down)


def _moe_combine_kernel(ys_ref, rt_ref, h_ref, m_ref, lng_ref, lnb_ref, o_ref, *, alpha):
    tm = h_ref.shape[0]
    rows = ys_ref.shape[0]
    ti = lax.broadcasted_iota(jnp.int32, (tm, tm), 0)
    tj = lax.broadcasted_iota(jnp.int32, (tm, tm), 1)
    eye = jnp.where(ti == tj, 1.0, 0.0).astype(BF16)
    cols = sum(lax.dot_general(eye, piece, (((1,), (1,)), ((), ())), preferred_element_type=F32)
               for piece in _split3(rt_ref[...]))
    lane = lax.broadcasted_iota(jnp.int32, (tm, rows), 1).astype(F32)
    ys = ys_ref[...]
    y = None
    for k in range(TOP_K):
        sel = jnp.where(lane == cols[:, TOP_K + k:TOP_K + k + 1], 1.0, 0.0).astype(BF16)
        part = cols[:, k:k + 1] * jnp.dot(sel, ys, preferred_element_type=F32)
        y = part if y is None else y + part
    o_ref[...] = _residual_ln(h_ref[...], y, m_ref, G2, lng_ref, lnb_ref, alpha)


def _moe_combine(ys, rt, h, mod, lng, lnb, tm, n_lat, batch, alpha, t_out):
    t, d = t_out, h.shape[1]
    rows = _local_rows(tm)
    row = pl.BlockSpec((tm, d), lambda i: (i, 0))
    return pl.pallas_call(
        functools.partial(_moe_combine_kernel, alpha=alpha),
        grid=(t // tm,),
        in_specs=[pl.BlockSpec((rows, d), lambda i: (i, 0)), pl.BlockSpec((SUBLANES, tm), lambda i: (0, i)),
                  row, _mod_spec(d, n_lat // tm, batch),
                  pl.BlockSpec((1, d), lambda i: (0, 0)), pl.BlockSpec((1, d), lambda i: (0, 0))],
        out_specs=row,
        out_shape=jax.ShapeDtypeStruct((t, d), F32),
        compiler_params=_cparams(("parallel",)),
        name="moe_combine_ln",
    )(ys, rt, h, mod, lng, lnb)


def _page_table(npg, pages_per_local):
    nb = npg.shape[0]
    ppb = PAGES_PER_BLOCK
    max_pages = nb * (pages_per_local - 1) + N_EXPERTS * (ppb - 1)
    nblk = -(-max_pages // ppb)
    lstart = jnp.cumsum(npg, 1) - npg
    pe = jnp.sum(npg, 0)
    pe_pad = (pe + ppb - 1) // ppb * ppb
    eend = jnp.cumsum(pe_pad)
    estart = eend - pe_pad
    starts = jnp.arange(nblk, dtype=jnp.int32) * ppb
    block_e = jnp.minimum(jnp.sum((eend[None, :] <= starts[:, None]).astype(jnp.int32), 1), N_EXPERTS - 1)
    n_used = (eend[-1:] // ppb).astype(jnp.int32)
    onehot = block_e[:, None] == jnp.arange(N_EXPERTS, dtype=jnp.int32)[None, :]
    pick = lambda m: jnp.sum(jnp.where(onehot[:, :, None], m.T[None], 0), 1)
    run_pages = pick(npg)
    run_slot = pick(estart[None, :] + jnp.cumsum(npg, 0) - npg)
    run_page = pick(jnp.arange(nb, dtype=jnp.int32)[:, None] * pages_per_local + lstart)
    s = (starts[:, None] + jnp.arange(ppb, dtype=jnp.int32)[None, :])[:, :, None]
    in_run = (run_slot[:, None, :] <= s) & (s < (run_slot + run_pages)[:, None, :])
    page = jnp.sum(jnp.where(in_run, (run_page - run_slot)[:, None, :] + s, 0), 2)
    slot_page = jnp.where(jnp.any(in_run, 2), page, -1).astype(jnp.int32).reshape(-1)
    return slot_page, block_e, n_used


def _moe_layer(h, mod, w_router, router_bias, e_gate, e_up, e_down, layer, lng, lnb, dims, alpha, t_out):
    batch, n_lat, n_ctx, tm = dims
    xs, rt, npg = _router(h, mod, w_router, router_bias, tm, n_lat, batch)
    npg = npg[:, 0].reshape(-1, N_EXPERTS).astype(jnp.int32)
    pages_per_local = _local_rows(tm) // PAGE
    slot_page, block_e, n_used = _page_table(npg, pages_per_local)
    ys = _experts(xs, slot_page, block_e, n_used, e_gate, e_up, e_down, layer, npg.shape[0] * pages_per_local)
    return _moe_combine(ys, rt, h, mod, lng, lnb, tm, n_lat, batch, alpha, t_out)


def kernel(x, c, ctx, c_ctx, w_ada, b_ada, ln_g, ln_b, a_wqkv, a_qg, a_kg, a_wo, b_win, b_vg, b_vb, b_ws, b_bs,
           b_wout, c_win, c_conv, c_alog, c_dtb, c_ng, c_wout, w_router, router_bias, e_gate, e_up, e_down):
    batch, n_lat, d = x.shape
    n_ctx = ctx.shape[1]
    depth = w_ada.shape[0]
    alpha = (2 * depth) ** 0.25
    assert batch + 1 <= SUBLANES
    tm = _pick_tile(512, n_lat, batch * n_ctx)
    dims = (batch, n_lat, n_ctx, tm)

    h = jnp.concatenate([x.reshape(batch * n_lat, d), ctx.reshape(batch * n_ctx, d)], 0).astype(F32)
    cond = jnp.concatenate([c, c_ctx[None], jnp.zeros((SUBLANES - batch - 1, d), c.dtype)], 0).astype(F32)
    ada = _ada_table(cond, w_ada, b_ada)
    mods = jnp.pad(ada.reshape(depth, SUBLANES, 6, d), ((0, 0), (0, 0), (0, SUBLANES - 6), (0, 0)))
    rope = _rope_tables(n_lat, tm)

    for i in range(depth):
        m, j = i % N_MIXERS, i // N_MIXERS
        mod = mods[i]
        lng1, lnb1 = ln_g[i, 0][None], ln_b[i, 0][None]
        lng2, lnb2 = ln_g[i, 1][None], ln_b[i, 1][None]
        if m == 0:
            h = _attention_layer(h, mod, a_wqkv[j], a_qg[j], a_kg[j], a_wo[j], lng1, lnb1, rope, dims, alpha)
        elif m == 1:
            h = _gmlp_layer(h, mod, b_win[j], b_vg[j], b_vb[j], b_ws[j], b_bs[j], b_wout[j], lng1, lnb1, dims, alpha)
        else:
            h = _deltanet_layer(h, mod, c_win[j], c_conv[j], c_alog[j], c_dtb[j], c_ng[j], c_wout[j],
                                lng1, lnb1, dims, alpha)
        t_out = h.shape[0] if i + 1 < depth else batch * n_lat
        h = _moe_layer(h, mod, w_router, router_bias, e_gate, e_up, e_down, i, lng2, lnb2, dims, alpha, t_out)
    return h.reshape(batch, n_lat, d).astype(x.dtype)
```

```python
import functools
import math

import jax
import jax.numpy as jnp
from jax import lax
from jax.experimental import pallas as pl
from jax.experimental.pallas import tpu as pltpu

F32 = jnp.float32
BF16 = jnp.bfloat16

GRID_W = 64
A_HEADS = 8
A_KV_HEADS = 2
A_GROUP = A_HEADS // A_KV_HEADS
HEAD_DIM = 128
ROPE_THETA = 10000.0
B_CHUNK = 128
B_GROUPS = 8
C_HEADS = 8
C_CONV = 3
C_CHUNK = 64
N_EXPERTS = 16
N_GROUPS = 4
TOP_K = 2
EPS = 1e-6
N_MIXERS = 3

LANES = 128
SUBLANES = 8
VMEM_LIMIT_BYTES = 56 * 1024 * 1024
LOG2E = 1.4426950408889634


def _cparams(sem, vmem=VMEM_LIMIT_BYTES):
    return pltpu.CompilerParams(dimension_semantics=sem, vmem_limit_bytes=vmem)


def _pick_tile(cap, *sizes):
    t = cap
    while t > SUBLANES and any(s % t for s in sizes):
        t //= 2
    assert all(s % t == 0 for s in sizes), (cap, sizes)
    return t


def _layer_norm(x, g, b):
    mu = jnp.mean(x, -1, keepdims=True)
    xc = x - mu
    var = jnp.mean(xc * xc, -1, keepdims=True)
    return xc * lax.rsqrt(var + EPS) * g + b


def _silu(x):
    return x * jax.nn.sigmoid(x)


def _bdot(a, b):
    return jnp.dot(a.astype(BF16), b.astype(BF16), preferred_element_type=F32)


def _bdot_nt(a, b):
    return lax.dot_general(a.astype(BF16), b.astype(BF16), (((1,), (1,)), ((), ())), preferred_element_type=F32)


def _split3(x):
    x1 = x.astype(BF16)
    r = x - x1.astype(F32)
    x2 = r.astype(BF16)
    x3 = (r - x2.astype(F32)).astype(BF16)
    return x1, x2, x3


def _ada_kernel(c_ref, w_ref, b_ref, o_ref):
    s = _silu(c_ref[...])
    o_ref[...] = jnp.dot(s, w_ref[...], preferred_element_type=F32, precision=lax.Precision.HIGHEST) + b_ref[...]


def _ada_table(cond, w_ada, b_ada):
    depth, d, d6 = w_ada.shape
    tn = _pick_tile(1536, d6)
    return pl.pallas_call(
        _ada_kernel,
        grid=(depth, d6 // tn),
        in_specs=[
            pl.BlockSpec((SUBLANES, d), lambda i, j: (0, 0)),
            pl.BlockSpec((None, d, tn), lambda i, j: (i, 0, j)),
            pl.BlockSpec((None, 1, tn), lambda i, j: (i, 0, j)),
        ],
        out_specs=pl.BlockSpec((None, SUBLANES, tn), lambda i, j: (i, 0, j)),
        out_shape=jax.ShapeDtypeStruct((depth, SUBLANES, d6), F32),
        compiler_params=_cparams(("arbitrary", "arbitrary")),
        name="ada_table",
    )(cond, w_ada, b_ada.reshape(depth, 1, d6))


SH1, SC1, G1, SH2, SC2, G2 = range(6)


def _mod_spec(d, blocks_per_batch, batch):
    return pl.BlockSpec((None, SUBLANES, d), lambda i, *_: (jnp.minimum(i // blocks_per_batch, batch), 0, 0))


def _modulate(h, m_ref, sh, sc):
    return h * (1.0 + m_ref[sc:sc + 1, :]) + m_ref[sh:sh + 1, :]


def _residual_ln(h, y, m_ref, gate, lng_ref, lnb_ref, alpha):
    return _layer_norm(alpha * h + m_ref[gate:gate + 1, :] * y, lng_ref[...], lnb_ref[...])


def _attn_proj_kernel(h_ref, m_ref, w_ref, qg_ref, kg_ref, cos_ref, sin_ref, q_ref, k_ref, v_ref):
    u = _modulate(h_ref[...], m_ref, SH1, SC1)
    qkv = _bdot(u, w_ref[...])
    cos = cos_ref[...]
    sin = sin_ref[...]
    nq = A_HEADS * HEAD_DIM
    nk = A_KV_HEADS * HEAD_DIM

    def norm_rope(xh, g):
        xn = xh * lax.rsqrt(jnp.mean(xh * xh, -1, keepdims=True) + EPS) * g
        return xn * cos + pltpu.roll(xn, HEAD_DIM // 2, 1) * sin

    for hd in range(A_HEADS):
        sl = slice(hd * HEAD_DIM, (hd + 1) * HEAD_DIM)
        q_ref[:, sl] = norm_rope(qkv[:, sl], qg_ref[...]).astype(BF16)
    for hd in range(A_KV_HEADS):
        sl = slice(hd * HEAD_DIM, (hd + 1) * HEAD_DIM)
        k_ref[:, sl] = norm_rope(qkv[:, nq + hd * HEAD_DIM:nq + (hd + 1) * HEAD_DIM], kg_ref[...]).astype(BF16)
    v_ref[...] = qkv[:, nq + nk:].astype(BF16)


def _attn_proj(h, mod, w_qkv, qg, kg, cos_t, sin_t, tm, n_lat, batch):
    t, d = h.shape
    nq = A_HEADS * HEAD_DIM
    nk = A_KV_HEADS * HEAD_DIM
    bpb = n_lat // tm

    def rope_map(i):
        return (jnp.where(i < batch * bpb, i % bpb, bpb), 0)

    return pl.pallas_call(
        _attn_proj_kernel,
        grid=(t // tm,),
        in_specs=[
            pl.BlockSpec((tm, d), lambda i: (i, 0)),
            _mod_spec(d, bpb, batch),
            pl.BlockSpec((d, nq + 2 * nk), lambda i: (0, 0)),
            pl.BlockSpec((1, HEAD_DIM), lambda i: (0, 0)),
            pl.BlockSpec((1, HEAD_DIM), lambda i: (0, 0)),
            pl.BlockSpec((tm, HEAD_DIM), rope_map),
            pl.BlockSpec((tm, HEAD_DIM), rope_map),
        ],
        out_specs=[
            pl.BlockSpec((tm, nq), lambda i: (i, 0)),
            pl.BlockSpec((tm, nk), lambda i: (i, 0)),
            pl.BlockSpec((tm, nk), lambda i: (i, 0)),
        ],
        out_shape=[
            jax.ShapeDtypeStruct((t, nq), BF16),
            jax.ShapeDtypeStruct((t, nk), BF16),
            jax.ShapeDtypeStruct((t, nk), BF16),
        ],
        compiler_params=_cparams(("parallel",)),
        name="attn_proj",
    )(h, mod, w_qkv, qg, kg, cos_t, sin_t)


def _flash_kernel(q_ref, k_ref, v_ref, o_ref, m_ref, l_ref, acc_ref, *, nchunk, tk):
    m_ref[...] = jnp.full(m_ref.shape, -jnp.inf, F32)
    l_ref[...] = jnp.zeros(l_ref.shape, F32)
    acc_ref[...] = jnp.zeros(acc_ref.shape, F32)
    nslab = tk // LANES

    def body(j, carry):
        off = pl.multiple_of(j * tk, tk)
        kj = k_ref[pl.ds(off, tk), :]
        vj = v_ref[pl.ds(off, tk), :]

        def scores(g):
            qg = q_ref[:, g * HEAD_DIM:(g + 1) * HEAD_DIM]
            return lax.dot_general(qg, kj, (((1,), (1,)), ((), ())), preferred_element_type=F32)

        sc_next = scores(0)
        for g in range(A_GROUP):
            sc = sc_next
            if g + 1 < A_GROUP:
                sc_next = scores(g + 1)
            slabs = [sc[:, c * LANES:(c + 1) * LANES] for c in range(nslab)]
            mc = slabs[0]
            for sl in slabs[1:]:
                mc = jnp.maximum(mc, sl)
            m_old = m_ref[g]
            m_new = jnp.maximum(m_old, jnp.max(mc, -1, keepdims=True))
            alpha = jnp.exp2(m_old - m_new)
            ps = [jnp.exp2(sl - m_new) for sl in slabs]
            lsum = ps[0]
            for x in ps[1:]:
                lsum = lsum + x
            p = jnp.concatenate([x.astype(BF16) for x in ps], axis=1)
            l_ref[g] = alpha * l_ref[g] + lsum
            acc_ref[g] = alpha * acc_ref[g] + jnp.dot(p, vj, preferred_element_type=F32)
            m_ref[g] = m_new
        return carry

    unroll = nchunk if nchunk <= 8 else next(u for u in (4, 3, 2, 1) if nchunk % u == 0)
    lax.fori_loop(0, nchunk, body, 0, unroll=unroll)
    for g in range(A_GROUP):
        out = acc_ref[g] / jnp.sum(l_ref[g], -1, keepdims=True)
        o_ref[:, g * HEAD_DIM:(g + 1) * HEAD_DIM] = out.astype(o_ref.dtype)


def _key_tile(n):
    for tk in (1408, 768, 512, 384, 256, 128):
        if n % tk == 0:
            return tk
    raise ValueError(n)


def _flash(q, k, v, *, batch, q_row0, q_len, tq):
    gw = A_GROUP * HEAD_DIM
    nqb = q_len // tq
    n_keys = k.shape[1]
    tk = _key_tile(n_keys)
    kv_spec = pl.BlockSpec((None, n_keys, HEAD_DIM), lambda b, kv, i: (b, 0, kv))
    return pl.pallas_call(
        functools.partial(_flash_kernel, nchunk=n_keys // tk, tk=tk),
        grid=(batch, A_KV_HEADS, nqb),
        in_specs=[pl.BlockSpec((tq, gw), lambda b, kv, i: (q_row0 // tq + b * nqb + i, kv)), kv_spec, kv_spec],
        out_specs=pl.BlockSpec((tq, gw), lambda b, kv, i: (b * nqb + i, kv)),
        out_shape=jax.ShapeDtypeStruct((batch * q_len, A_HEADS * HEAD_DIM), BF16),
        scratch_shapes=[pltpu.VMEM((A_GROUP, tq, LANES), F32)] * 2 + [pltpu.VMEM((A_GROUP, tq, HEAD_DIM), F32)],
        compiler_params=_cparams(("parallel", "parallel", "arbitrary")),
        name="flash_attn",
    )(q, k, v)


def _out_proj_kernel(al_ref, ac_ref, w_ref, h_ref, m_ref, lng_ref, lnb_ref, o_ref, *, alpha, lat_blocks):
    a = jnp.where(pl.program_id(0) < lat_blocks, al_ref[...], ac_ref[...])
    y = jnp.dot(a, w_ref[...], preferred_element_type=F32)
    o_ref[...] = _residual_ln(h_ref[...], y, m_ref, G1, lng_ref, lnb_ref, alpha)


def _out_proj(a_lat, a_ctx, w, h, mod, lng, lnb, tm, n_lat, batch, alpha):
    t, d = h.shape
    ka = a_lat.shape[1]
    lat_blocks = a_lat.shape[0] // tm
    assert a_lat.shape[0] % tm == 0 and a_ctx.shape[0] % tm == 0
    return pl.pallas_call(
        functools.partial(_out_proj_kernel, alpha=alpha, lat_blocks=lat_blocks),
        grid=(t // tm,),
        in_specs=[
            pl.BlockSpec((tm, ka), lambda i: (jnp.minimum(i, lat_blocks - 1), 0)),
            pl.BlockSpec((tm, ka), lambda i: (jnp.maximum(i - lat_blocks, 0), 0)),
            pl.BlockSpec((ka, d), lambda i: (0, 0)),
            pl.BlockSpec((tm, d), lambda i: (i, 0)),
            _mod_spec(d, n_lat // tm, batch),
            pl.BlockSpec((1, d), lambda i: (0, 0)),
            pl.BlockSpec((1, d), lambda i: (0, 0)),
        ],
        out_specs=pl.BlockSpec((tm, d), lambda i: (i, 0)),
        out_shape=jax.ShapeDtypeStruct((t, d), F32),
        compiler_params=_cparams(("parallel",)),
        name="out_proj_ln",
    )(a_lat, a_ctx, w, h, mod, lng, lnb)


def _rope_tables(n, tm):
    rows = n // GRID_W
    half = HEAD_DIM // 2
    inv = 1.0 / (ROPE_THETA ** (jnp.arange(0, half, 2, dtype=F32) / half))
    r = jnp.broadcast_to(jnp.arange(rows, dtype=F32)[:, None], (rows, GRID_W)).reshape(-1)
    col = jnp.broadcast_to(jnp.arange(GRID_W, dtype=F32)[None, :], (rows, GRID_W)).reshape(-1)
    ang = jnp.concatenate([r[:, None] * inv, col[:, None] * inv], -1)
    cos, sin = jnp.cos(ang), jnp.sin(ang)
    cos_t = jnp.concatenate([jnp.concatenate([cos, cos], -1), jnp.ones((tm, HEAD_DIM), F32)], 0)
    sin_t = jnp.concatenate([jnp.concatenate([-sin, sin], -1), jnp.zeros((tm, HEAD_DIM), F32)], 0)
    return cos_t, sin_t


def _deinterleave_heads(w, n_heads):
    lead = w.shape[:-1]
    w = w.reshape(*lead, n_heads, HEAD_DIM // 2, 2)
    return jnp.swapaxes(w, -1, -2).reshape(*lead, n_heads * HEAD_DIM)


def _attention_layer(h, mod, w_qkv, q_g, k_g, w_o, lng, lnb, rope, dims, alpha):
    batch, n_lat, n_ctx, tm = dims
    nq = A_HEADS * HEAD_DIM
    nk = A_KV_HEADS * HEAD_DIM
    w = jnp.concatenate([
        _deinterleave_heads(w_qkv[:, :nq], A_HEADS),
        _deinterleave_heads(w_qkv[:, nq:nq + nk], A_KV_HEADS),
        w_qkv[:, nq + nk:]], -1).astype(BF16)
    qg = _deinterleave_heads(q_g[None, :], 1) * (HEAD_DIM ** -0.5 * LOG2E)
    kg = _deinterleave_heads(k_g[None, :], 1)
    q, k, v = _attn_proj(h, mod, w, qg, kg, rope[0], rope[1], tm, n_lat, batch)
    lat0 = batch * n_lat
    kvw = A_KV_HEADS * HEAD_DIM
    k_ctx, v_ctx = k[lat0:].reshape(batch, n_ctx, kvw), v[lat0:].reshape(batch, n_ctx, kvw)
    k_all = jnp.concatenate([k_ctx, k[:lat0].reshape(batch, n_lat, kvw)], 1)
    v_all = jnp.concatenate([v_ctx, v[:lat0].reshape(batch, n_lat, kvw)], 1)
    o_lat = _flash(q, k_all, v_all, batch=batch, q_row0=0, q_len=n_lat, tq=_pick_tile(512, n_lat))
    o_ctx = _flash(q, k_ctx, v_ctx, batch=batch, q_row0=lat0, q_len=n_ctx, tq=_pick_tile(256, n_ctx))
    return _out_proj(o_lat, o_ctx, w_o.astype(BF16), h, mod, lng, lnb, tm, n_lat, batch, alpha)


def _gmlp_kernel(h_ref, m_ref, win_ref, vg_ref, vb_ref, ws_ref, bs_ref, wout_ref, lng_ref, lnb_ref, o_ref,
                 gated_ref, *, alpha, width):
    h = h_ref[...]
    u = _modulate(h, m_ref, SH1, SC1)
    z = jax.nn.gelu(_bdot(u, win_ref[...]))
    uu = z[:, :width]
    v = _layer_norm(z[:, width:], vg_ref[...], vb_ref[...]).astype(BF16)
    tm = h.shape[0]
    gw = width // B_GROUPS
    for c in range(tm // B_CHUNK):
        rs = slice(c * B_CHUNK, (c + 1) * B_CHUNK)
        for g in range(B_GROUPS):
            cs = slice(g * gw, (g + 1) * gw)
            mixed = jnp.dot(ws_ref[g], v[rs, cs], preferred_element_type=F32) + bs_ref[g]
            gated_ref[rs, cs] = (uu[rs, cs] * mixed).astype(BF16)
    y = jnp.dot(gated_ref[...], wout_ref[...], preferred_element_type=F32)
    o_ref[...] = _residual_ln(h, y, m_ref, G1, lng_ref, lnb_ref, alpha)


def _gmlp_layer(h, mod, w_in, v_g, v_b, w_s, b_s, w_out, lng, lnb, dims, alpha):
    batch, n_lat, n_ctx, tm = dims
    t, d = h.shape
    width = w_out.shape[0]
    gw = width // B_GROUPS
    assert tm % B_CHUNK == 0 and n_ctx % B_CHUNK == 0
    bs = jnp.broadcast_to(b_s[:, :, None], (B_GROUPS, B_CHUNK, gw)).astype(F32)
    const = lambda *shape: pl.BlockSpec(shape, lambda i: (0,) * len(shape))
    return pl.pallas_call(
        functools.partial(_gmlp_kernel, alpha=alpha, width=width),
        grid=(t // tm,),
        in_specs=[
            pl.BlockSpec((tm, d), lambda i: (i, 0)),
            _mod_spec(d, n_lat // tm, batch),
            const(d, 2 * width), const(1, width), const(1, width),
            const(B_GROUPS, B_CHUNK, B_CHUNK), const(B_GROUPS, B_CHUNK, gw),
            const(width, d), const(1, d), const(1, d),
        ],
        out_specs=pl.BlockSpec((tm, d), lambda i: (i, 0)),
        out_shape=jax.ShapeDtypeStruct((t, d), F32),
        scratch_shapes=[pltpu.VMEM((tm, width), BF16)],
        compiler_params=_cparams(("parallel",)),
        name="gmlp_layer",
    )(h, mod, w_in.astype(BF16), v_g[None], v_b[None], w_s.astype(BF16), bs, w_out.astype(BF16), lng, lnb)


def _delta_proj_kernel(h_ref, m_ref, w_ref, wg_ref, o_ref, og_ref, *, tn):
    u = _modulate(h_ref[...], m_ref, SH1, SC1).astype(BF16)
    for j in range(w_ref.shape[1] // tn):
        o_ref[:, j * tn:(j + 1) * tn] = jnp.dot(u, w_ref[:, j * tn:(j + 1) * tn], preferred_element_type=F32)
    og_ref[...] = jnp.dot(u, wg_ref[...], preferred_element_type=F32)


def _delta_proj(h, mod, w, tm, n_lat, batch):
    t, d = h.shape
    ncol = w.shape[1] - LANES
    return pl.pallas_call(
        functools.partial(_delta_proj_kernel, tn=_pick_tile(1024, ncol)),
        grid=(t // tm,),
        in_specs=[
            pl.BlockSpec((tm, d), lambda i: (i, 0)),
            _mod_spec(d, n_lat // tm, batch),
            pl.BlockSpec((d, ncol), lambda i: (0, 0)),
            pl.BlockSpec((d, LANES), lambda i: (0, 0)),
        ],
        out_specs=[pl.BlockSpec((tm, ncol), lambda i: (i, 0)), pl.BlockSpec((tm, LANES), lambda i: (i, 0))],
        out_shape=[jax.ShapeDtypeStruct((t, ncol), F32), jax.ShapeDtypeStruct((t, LANES), F32)],
        compiler_params=_cparams(("parallel",)),
        name="delta_proj",
    )(h, mod, w[:, :ncol], w[:, ncol:])


def _delta_prep_kernel(x_ref, xp_ref, xn_ref, ab_ref, cw_ref, ga_ref, dtb_ref, q_ref, k_ref, v_ref, gb_ref,
                       *, tm, n_lat_rows, n_lat, n_ctx, cw):
    i = pl.program_id(0)
    row0 = i * tm
    in_lat = row0 < n_lat_rows
    pos = jnp.where(in_lat, row0 % n_lat, (row0 - n_lat_rows) % n_ctx)
    seq = jnp.where(in_lat, n_lat, n_ctx)
    has_prev = (pos != 0).astype(F32)
    has_next = (pos + tm != seq).astype(F32)
    rows = lax.broadcasted_iota(jnp.int32, (tm, HEAD_DIM), 0)
    first = rows == 0
    last = rows == tm - 1
    heads = cw // HEAD_DIM
    for part, out_ref in enumerate((q_ref, k_ref, v_ref)):
        for hd in range(heads):
            sl = slice(part * cw + hd * HEAD_DIM, part * cw + (hd + 1) * HEAD_DIM)
            x = x_ref[:, sl]
            prev_row = xp_ref[SUBLANES - 1:SUBLANES, sl] * has_prev
            next_row = xn_ref[0:1, sl] * has_next
            x_prev = jnp.where(first, prev_row, pltpu.roll(x, 1, 0))
            x_next = jnp.where(last, next_row, pltpu.roll(x, tm - 1, 0))
            y = _silu(x_prev * cw_ref[0:1, sl] + x * cw_ref[1:2, sl] + x_next * cw_ref[2:3, sl])
            if part < 2:
                y = y * lax.rsqrt(jnp.sum(y * y, -1, keepdims=True) + EPS)
            if part == 0:
                y = y * (HEAD_DIM ** -0.5)
            out_ref[:, hd * HEAD_DIM:(hd + 1) * HEAD_DIM] = y.astype(out_ref.dtype)
    ab = ab_ref[...]
    xg = ab + dtb_ref[...]
    softplus = jnp.maximum(xg, 0.0) + jnp.log1p(jnp.exp(-jnp.abs(xg)))
    lane = lax.broadcasted_iota(jnp.int32, ab.shape, 1)
    gb_ref[...] = jnp.where(lane < 2 * C_HEADS, ga_ref[...] * softplus, jax.nn.sigmoid(ab))


def _delta_prep(proj, ab, conv_w, a_log, dt_bias, tm, n_lat, n_ctx, batch):
    t = proj.shape[0]
    cw = C_HEADS * HEAD_DIM
    r8 = tm // SUBLANES
    nb8 = t // SUBLANES
    ga = jnp.zeros((1, LANES), F32).at[0, :2 * C_HEADS].set(-jnp.exp(a_log.astype(F32)).reshape(-1))
    dtb = jnp.zeros((1, LANES), F32).at[0, :2 * C_HEADS].set(dt_bias.astype(F32).reshape(-1))
    cwp = jnp.zeros((SUBLANES, 3 * cw), F32).at[:C_CONV].set(conv_w)
    return pl.pallas_call(
        functools.partial(_delta_prep_kernel, tm=tm, n_lat_rows=batch * n_lat, n_lat=n_lat, n_ctx=n_ctx, cw=cw),
        grid=(t // tm,),
        in_specs=[
            pl.BlockSpec((tm, 3 * cw), lambda i: (i, 0)),
            pl.BlockSpec((SUBLANES, 3 * cw), lambda i: (jnp.maximum(i * r8 - 1, 0), 0)),
            pl.BlockSpec((SUBLANES, 3 * cw), lambda i: (jnp.minimum((i + 1) * r8, nb8 - 1), 0)),
            pl.BlockSpec((tm, LANES), lambda i: (i, 0)),
            pl.BlockSpec((SUBLANES, 3 * cw), lambda i: (0, 0)),
            pl.BlockSpec((1, LANES), lambda i: (0, 0)),
            pl.BlockSpec((1, LANES), lambda i: (0, 0)),
        ],
        out_specs=[pl.BlockSpec((tm, cw), lambda i: (i, 0))] * 3 + [pl.BlockSpec((tm, LANES), lambda i: (i, 0))],
        out_shape=[jax.ShapeDtypeStruct((t, cw), BF16)] * 3 + [jax.ShapeDtypeStruct((t, LANES), F32)],
        compiler_params=_cparams(("parallel",)),
        name="delta_prep",
    )(proj, proj, proj, ab, cwp, ga, dtb)


DELTA_BLOCK = 16


def _delta_scan_kernel(*refs):
    nst = 2
    in_refs = [refs[4 * st:4 * st + 4] for st in range(nst)]
    out_refs = refs[4 * nst:5 * nst]
    s_ref, wq_s, u_s, at_s, kd_s, gl_s = refs[5 * nst:]
    c = pl.program_id(1)

    @pl.when(c == 0)
    def _():
        for ref in (s_ref, wq_s, u_s, at_s, kd_s, gl_s):
            ref[...] = jnp.zeros(ref.shape, ref.dtype)

    L = C_CHUNK
    ri = lax.broadcasted_iota(jnp.int32, (L, L), 0)
    ci = lax.broadcasted_iota(jnp.int32, (L, L), 1)
    eye = (ri == ci).astype(F32)
    blk_mask = (ri // DELTA_BLOCK) == (ci // DELTA_BLOCK)
    dirs = []
    for st, (q_ref, k_ref, v_ref, g_ref) in enumerate(in_refs):
        direction = st % 2
        tri, strict = (ri >= ci, ri > ci) if direction == 0 else (ri <= ci, ri < ci)
        tri_b = tri.astype(BF16)
        gb = g_ref[...]
        g1, g2, g3 = _split3(gb)
        gc = (jnp.dot(tri_b, g1, preferred_element_type=F32) + jnp.dot(tri_b, g2, preferred_element_type=F32)
              + jnp.dot(tri_b, g3, preferred_element_type=F32))
        dirs.append((q_ref, k_ref, v_ref, gb, gc, gc.T, tri, strict, L - 1 if direction == 0 else 0))

    chains = [(st, hd) for st in range(nst) for hd in range(C_HEADS)]

    def gate_lane(d, hd):
        return (d % 2) * C_HEADS + hd
    nch = len(chains)

    def per_chain(fn):
        return [fn(i, *chains[i]) for i in range(nch)]

    def lanes(hd):
        return slice(hd * HEAD_DIM, (hd + 1) * HEAD_DIM)

    s_old = per_chain(lambda i, d, hd: s_ref[i])
    ws = per_chain(lambda i, d, hd: jnp.dot(wq_s[i], s_old[i].astype(BF16), preferred_element_type=F32))

    kh = per_chain(lambda i, d, hd: dirs[d][1][:, lanes(hd)])
    qh = per_chain(lambda i, d, hd: dirs[d][0][:, lanes(hd)])
    kq = per_chain(lambda i, d, hd: _bdot_nt(jnp.concatenate([kh[i], qh[i]], 0), kh[i]))
    g_col = per_chain(lambda i, d, hd: dirs[d][4][:, gate_lane(d, hd):gate_lane(d, hd) + 1])
    g_row = per_chain(lambda i, d, hd: dirs[d][5][gate_lane(d, hd):gate_lane(d, hd) + 1, :])
    beta = per_chain(lambda i, d, hd: dirs[d][3][:, 2 * C_HEADS + gate_lane(d, hd):2 * C_HEADS + gate_lane(d, hd) + 1])
    g_last = per_chain(lambda i, d, hd: g_col[i][dirs[d][8]:dirs[d][8] + 1, :])
    decay = per_chain(lambda i, d, hd: jnp.exp(jnp.where(dirs[d][6], g_col[i] - g_row[i], -jnp.inf)))
    a_mat = per_chain(lambda i, d, hd: jnp.where(dirs[d][7], kq[i][:L] * beta[i] * decay[i], 0.0))
    attn = per_chain(lambda i, d, hd: kq[i][L:] * decay[i])
    eg = per_chain(lambda i, d, hd: jnp.exp(g_col[i]))
    rhs = per_chain(lambda i, d, hd: jnp.concatenate(
        [dirs[d][2][:, lanes(hd)].astype(F32) * beta[i], kh[i].astype(F32) * (beta[i] * eg[i])], 1))
    q_dec = per_chain(lambda i, d, hd: qh[i].astype(F32) * eg[i])
    k_dec = per_chain(lambda i, d, hd: kh[i].astype(F32) * jnp.exp(g_last[i] - g_col[i]))

    v_new = per_chain(lambda i, d, hd: (u_s[i] - ws[i][:L]).astype(BF16))
    o = per_chain(lambda i, d, hd: ws[i][L:] + jnp.dot(at_s[i], v_new[i], preferred_element_type=F32))
    kv = per_chain(lambda i, d, hd: lax.dot_general(
        kd_s[i], v_new[i], (((0,), (0,)), ((), ())), preferred_element_type=F32))
    for i, (d, hd) in enumerate(chains):
        s_ref[i] = s_old[i] * gl_s[i][0:1, :] + kv[i]
        out_refs[d][:, lanes(hd)] = o[i].astype(out_refs[d].dtype)

    n = per_chain(lambda i, d, hd: jnp.where(blk_mask, a_mat[i], 0.0))
    off = per_chain(lambda i, d, hd: a_mat[i] - n[i])
    n2 = per_chain(lambda i, d, hd: _bdot(n[i], n[i]))
    n4 = per_chain(lambda i, d, hd: _bdot(n2[i], n2[i]))
    dv = per_chain(lambda i, d, hd: eye - n[i] + n2[i] - _bdot(n[i], n2[i]))
    n8 = per_chain(lambda i, d, hd: _bdot(n4[i], n4[i]))
    dv = per_chain(lambda i, d, hd: dv[i] + _bdot(dv[i], n4[i]))
    dv = per_chain(lambda i, d, hd: dv[i] + _bdot(dv[i], n8[i]))
    m = per_chain(lambda i, d, hd: _bdot(dv[i], off[i]))
    r1 = per_chain(lambda i, d, hd: _bdot(dv[i], rhs[i]))
    m2 = per_chain(lambda i, d, hd: _bdot(m[i], m[i]))
    x1 = per_chain(lambda i, d, hd: r1[i] - _bdot(m[i], r1[i]))
    uw = per_chain(lambda i, d, hd: x1[i] + _bdot(m2[i], x1[i]))

    for i in range(nch):
        wq_s[i] = jnp.concatenate([uw[i][:, HEAD_DIM:], q_dec[i]], 0).astype(BF16)
        u_s[i] = uw[i][:, :HEAD_DIM]
        at_s[i] = attn[i].astype(BF16)
        kd_s[i] = k_dec[i].astype(BF16)
        gl_s[i] = jnp.broadcast_to(jnp.exp(g_last[i]), gl_s.shape[1:])


def _delta_scan(q, k, v, gb, n_lat, n_ctx, batch):
    t, cw = q.shape
    nlc, ncc = n_lat // C_CHUNK, n_ctx // C_CHUNK
    lat_blocks = batch * nlc

    def fwd_map(b, c):
        return jnp.where(c < ncc, lat_blocks + b * ncc + c, b * nlc + (c - ncc))

    def bwd_map(b, c):
        return jnp.where(c < ncc, lat_blocks + b * ncc + (ncc - 1 - c), b * nlc + (nlc - 1 - (c - ncc)))

    nch = ncc + nlc
    fin = lambda w: pl.BlockSpec((C_CHUNK, w), lambda b, c: (fwd_map(b, jnp.minimum(c, nch - 1)), 0))
    bin_ = lambda w: pl.BlockSpec((C_CHUNK, w), lambda b, c: (bwd_map(b, jnp.minimum(c, nch - 1)), 0))
    fout = pl.BlockSpec((C_CHUNK, cw), lambda b, c: (fwd_map(b, jnp.maximum(c - 1, 0)), 0))
    bout = pl.BlockSpec((C_CHUNK, cw), lambda b, c: (bwd_map(b, jnp.maximum(c - 1, 0)), 0))
    nchain = 2 * C_HEADS
    return pl.pallas_call(
        _delta_scan_kernel,
        grid=(batch, nch + 1),
        in_specs=[fin(cw), fin(cw), fin(cw), fin(LANES), bin_(cw), bin_(cw), bin_(cw), bin_(LANES)],
        out_specs=[fout, bout],
        out_shape=[jax.ShapeDtypeStruct((t, cw), BF16)] * 2,
        scratch_shapes=[
            pltpu.VMEM((nchain, HEAD_DIM, HEAD_DIM), F32),
            pltpu.VMEM((nchain, 2 * C_CHUNK, HEAD_DIM), BF16),
            pltpu.VMEM((nchain, C_CHUNK, HEAD_DIM), F32),
            pltpu.VMEM((nchain, C_CHUNK, C_CHUNK), BF16),
            pltpu.VMEM((nchain, C_CHUNK, HEAD_DIM), BF16),
            pltpu.VMEM((nchain, SUBLANES, LANES), F32),
        ],
        compiler_params=_cparams(("parallel", "arbitrary")),
        name="delta_scan",
    )(q, k, v, gb, q, k, v, gb)


def _delta_out_kernel(of_ref, ob_ref, z_ref, ng_ref, w_ref, h_ref, m_ref, lng_ref, lnb_ref, o_ref, y_ref, *, alpha):
    for hd in range(C_HEADS):
        sl = slice(hd * HEAD_DIM, (hd + 1) * HEAD_DIM)
        o = of_ref[:, sl].astype(F32) + ob_ref[:, sl].astype(F32)
        on = o * lax.rsqrt(jnp.mean(o * o, -1, keepdims=True) + EPS) * ng_ref[...]
        y_ref[:, sl] = (on * _silu(z_ref[:, sl])).astype(BF16)
    y = jnp.dot(y_ref[...], w_ref[...], preferred_element_type=F32)
    o_ref[...] = _residual_ln(h_ref[...], y, m_ref, G1, lng_ref, lnb_ref, alpha)


def _delta_out(o_f, o_b, proj, norm_g, w_out, h, mod, lng, lnb, tm, n_lat, batch, alpha):
    t, d = h.shape
    cw = C_HEADS * HEAD_DIM
    row = lambda w: pl.BlockSpec((tm, w), lambda i: (i, 0))
    return pl.pallas_call(
        functools.partial(_delta_out_kernel, alpha=alpha),
        grid=(t // tm,),
        in_specs=[
            row(cw), row(cw),
            pl.BlockSpec((tm, cw), lambda i: (i, 3)),
            pl.BlockSpec((1, HEAD_DIM), lambda i: (0, 0)),
            pl.BlockSpec((cw, d), lambda i: (0, 0)),
            row(d),
            _mod_spec(d, n_lat // tm, batch),
            pl.BlockSpec((1, d), lambda i: (0, 0)),
            pl.BlockSpec((1, d), lambda i: (0, 0)),
        ],
        out_specs=row(d),
        out_shape=jax.ShapeDtypeStruct((t, d), F32),
        scratch_shapes=[pltpu.VMEM((tm, cw), BF16)],
        compiler_params=_cparams(("parallel",)),
        name="delta_out_ln",
    )(o_f, o_b, proj, norm_g[None], w_out, h, mod, lng, lnb)


def _deltanet_layer(h, mod, w_in, conv_w, a_log, dt_bias, norm_g, w_out, lng, lnb, dims, alpha):
    batch, n_lat, n_ctx, tm = dims
    cw = C_HEADS * HEAD_DIM
    d = h.shape[1]
    n_gate = w_in.shape[1] - 4 * cw
    w = jnp.concatenate([w_in, jnp.zeros((d, LANES - n_gate), w_in.dtype)], 1).astype(BF16)
    proj, ab = _delta_proj(h, mod, w, tm, n_lat, batch)
    tmc = _pick_tile(256, n_lat, n_ctx)
    q, k, v, gb = _delta_prep(proj, ab, conv_w, a_log, dt_bias, tmc, n_lat, n_ctx, batch)
    o_f, o_b = _delta_scan(q, k, v, gb, n_lat, n_ctx, batch)
    return _delta_out(o_f, o_b, proj, norm_g, w_out.astype(BF16), h, mod, lng, lnb, tm, n_lat, batch, alpha)


PAGE = 16
PAGES_PER_BLOCK = 32


def _local_rows(tm):
    return TOP_K * tm + N_EXPERTS * PAGE


def _spare_blocks(tm):
    return -(-(2 * PAGES_PER_BLOCK + 1) // (_local_rows(tm) // PAGE))


def _router_kernel(h_ref, m_ref, wr_ref, bias_ref, xs_ref, rt_ref, npg_ref, *, nb):
    i = pl.program_id(0)

    @pl.when(i < nb)
    def _():
        _route_block(h_ref, m_ref, wr_ref, bias_ref, xs_ref, rt_ref, npg_ref)

    @pl.when(i >= nb)
    def _():
        xs_ref[...] = jnp.zeros(xs_ref.shape, xs_ref.dtype)


def _route_block(h_ref, m_ref, wr_ref, bias_ref, xs_ref, rt_ref, npg_ref):
    u = _modulate(h_ref[...], m_ref, SH2, SC2)
    tm = u.shape[0]
    u1, u2, _ = _split3(u)
    w1, w2, _ = _split3(wr_ref[...])
    nt = lambda a, b: lax.dot_general(a, b, (((1,), (1,)), ((), ())), preferred_element_type=F32)
    logits = nt(w1, u1) + nt(w1, u2) + nt(w2, u1)
    scores = jax.nn.sigmoid(logits)
    sel = scores + bias_ref[:, 0:1]
    per = N_EXPERTS // N_GROUPS
    best = None
    gidx = jnp.zeros((1, tm), jnp.int32)
    for g in range(N_GROUPS):
        r = [sel[g * per + j:g * per + j + 1, :] for j in range(per)]
        top2 = None
        for a in range(per):
            for b in range(a + 1, per):
                pair = r[a] + r[b]
                top2 = pair if top2 is None else jnp.maximum(top2, pair)
        if best is None:
            best = top2
        else:
            better = top2 > best
            gidx = jnp.where(better, g, gidx)
            best = jnp.where(better, top2, best)
    eio = lax.broadcasted_iota(jnp.int32, (N_EXPERTS, tm), 0)
    masked = jnp.where(eio // per == gidx, sel, -jnp.inf)
    m1 = jnp.max(masked, 0, keepdims=True)
    i1 = jnp.min(jnp.where(masked == m1, eio, N_EXPERTS), 0, keepdims=True)
    masked2 = jnp.where(eio == i1, -jnp.inf, masked)
    m2 = jnp.max(masked2, 0, keepdims=True)
    i2 = jnp.min(jnp.where(masked2 == m2, eio, N_EXPERTS), 0, keepdims=True)
    oh1 = eio == i1
    oh2 = eio == i2
    s1 = jnp.sum(jnp.where(oh1, scores, 0.0), 0, keepdims=True)
    s2 = jnp.sum(jnp.where(oh2, scores, 0.0), 0, keepdims=True)
    tot = s1 + s2
    oh = jnp.where(oh1, 1.0, jnp.where(oh2, 1.0, 0.0))
    ti = lax.broadcasted_iota(jnp.int32, (tm, tm), 0)
    tj = lax.broadcasted_iota(jnp.int32, (tm, tm), 1)
    before = jnp.where(ti < tj, 1.0, 0.0).astype(BF16)
    rank = jnp.dot(oh.astype(BF16), before, preferred_element_type=F32)
    cnt = jnp.sum(oh, 1, keepdims=True)
    npg = jnp.floor((cnt + (PAGE - 1)) * (1.0 / PAGE))
    ei = lax.broadcasted_iota(jnp.int32, (N_EXPERTS, N_EXPERTS), 0)
    ej = lax.broadcasted_iota(jnp.int32, (N_EXPERTS, N_EXPERTS), 1)
    lower = jnp.where(ej < ei, 1.0, 0.0).astype(BF16)
    npg_b = jnp.broadcast_to(npg, (N_EXPERTS, LANES))
    start_pg = jnp.dot(lower, npg_b.astype(BF16), preferred_element_type=F32)
    pos = start_pg[:, 0:1] * PAGE + rank
    p1 = jnp.sum(jnp.where(oh1, pos, 0.0), 0, keepdims=True)
    p2 = jnp.sum(jnp.where(oh2, pos, 0.0), 0, keepdims=True)
    rows = xs_ref.shape[0]
    rc = 2 * LANES
    assert rows % rc == 0
    ub = u1
    for c in range(rows // rc):
        rio = (lax.broadcasted_iota(jnp.int32, (rc, tm), 0) + c * rc).astype(F32)
        perm = jnp.where(rio == p1, 1.0, jnp.where(rio == p2, 1.0, 0.0)).astype(BF16)
        xs_ref[c * rc:(c + 1) * rc, :] = jnp.dot(perm, ub, preferred_element_type=F32).astype(BF16)
    rt_ref[...] = jnp.concatenate([s1 / tot, s2 / tot, p1, p2, jnp.zeros((SUBLANES - 4, tm), F32)], 0)
    npg_ref[...] = npg_b


def _router(h, mod, w_router, router_bias, tm, n_lat, batch):
    t, d = h.shape
    nb = t // tm
    rows = _local_rows(tm)
    wr = w_router.T.astype(F32)
    bias = jnp.broadcast_to(router_bias.astype(F32)[:, None], (N_EXPERTS, LANES))
    last = nb - 1
    ngrid = nb + _spare_blocks(tm)
    return pl.pallas_call(
        functools.partial(_router_kernel, nb=nb),
        grid=(ngrid,),
        in_specs=[
            pl.BlockSpec((tm, d), lambda i: (jnp.minimum(i, last), 0)),
            _mod_spec(d, n_lat // tm, batch),
            pl.BlockSpec((N_EXPERTS, d), lambda i: (0, 0)),
            pl.BlockSpec((N_EXPERTS, LANES), lambda i: (0, 0)),
        ],
        out_specs=[
            pl.BlockSpec((rows, d), lambda i: (i, 0)),
            pl.BlockSpec((SUBLANES, tm), lambda i: (0, jnp.minimum(i, last))),
            pl.BlockSpec((N_EXPERTS, LANES), lambda i: (jnp.minimum(i, last), 0)),
        ],
        out_shape=[
            jax.ShapeDtypeStruct((ngrid * rows, d), BF16),
            jax.ShapeDtypeStruct((SUBLANES, t), F32),
            jax.ShapeDtypeStruct((nb * N_EXPERTS, LANES), F32),
        ],
        compiler_params=_cparams(("arbitrary",)),
        name="moe_router",
    )(h, mod, wr, bias)


def _expert_kernel(pg_ref, be_ref, nu_ref, xs_in_hbm, wg_ref, wu_ref, wd_ref, xs_hbm,
                   xbuf, ybuf, wg_s, wu_s, wd_s, in_sem, out_sem, *, nblk, spare_page0):
    del xs_in_hbm
    ys_hbm = xs_hbm
    i = pl.program_id(0)
    n_used = nu_ref[0]

    zero_page = spare_page0 + 2 * PAGES_PER_BLOCK

    def in_copy(blk, buf, j):
        page = pg_ref[blk * PAGES_PER_BLOCK + j]
        page = jnp.where(page >= 0, page, zero_page)
        return pltpu.make_async_copy(xs_hbm.at[pl.ds(pl.multiple_of(page * PAGE, PAGE), PAGE)],
                                     xbuf.at[buf, pl.ds(j * PAGE, PAGE)], in_sem.at[buf])

    def out_copy(blk, buf, j):
        page = pg_ref[blk * PAGES_PER_BLOCK + j]
        page = jnp.where(page >= 0, page, spare_page0 + buf * PAGES_PER_BLOCK + j)
        return pltpu.make_async_copy(ybuf.at[buf, pl.ds(j * PAGE, PAGE)],
                                     ys_hbm.at[pl.ds(pl.multiple_of(page * PAGE, PAGE), PAGE)], out_sem.at[buf])

    def fetch(blk, buf):
        for j in range(PAGES_PER_BLOCK):
            in_copy(blk, buf, j).start()

    def wait_out(blk, buf):
        for j in range(PAGES_PER_BLOCK):
            out_copy(blk, buf, j).wait()

    buf = i % 2

    @pl.when(jnp.logical_and(i == 0, n_used > 0))
    def _():
        fetch(0, 0)

    @pl.when(i + 1 < n_used)
    def _():
        fetch(i + 1, 1 - buf)

    @pl.when(i < n_used)
    def _():
        for j in range(PAGES_PER_BLOCK):
            in_copy(i, buf, j).wait()

        @pl.when(i >= 2)
        def _():
            wait_out(i - 2, buf)

        @pl.when(jnp.logical_or(i == 0, be_ref[i] != be_ref[jnp.maximum(i - 1, 0)]))
        def _():
            wg_s[...] = wg_ref[...].astype(BF16)
            wu_s[...] = wu_ref[...].astype(BF16)
            wd_s[...] = wd_ref[...].astype(BF16)

        half = xbuf.shape[1] // 2
        xs_ = [xbuf[buf, pl.ds(r * half, half), :] for r in range(2)]
        gate = [jnp.dot(x, wg_s[...], preferred_element_type=F32) for x in xs_]
        up = [jnp.dot(x, wu_s[...], preferred_element_type=F32) for x in xs_]
        for r in range(2):
            hid = (_silu(gate[r]) * up[r]).astype(BF16)
            ybuf[buf, pl.ds(r * half, half), :] = jnp.dot(
                hid, wd_s[...], preferred_element_type=F32).astype(ybuf.dtype)
        for j in range(PAGES_PER_BLOCK):
            out_copy(i, buf, j).start()

    @pl.when(i == nblk - 1)
    def _():
        for back in (2, 1):
            @pl.when(n_used - back >= 0)
            def _():
                wait_out(n_used - back, (n_used - back) % 2)


def _experts(xs, slot_page, block_e, n_used, e_gate, e_up, e_down, layer, spare_page0):
    n_rows, d = xs.shape
    assert n_rows // PAGE - spare_page0 > 2 * PAGES_PER_BLOCK
    de = e_gate.shape[-1]
    nblk = block_e.shape[0]
    rows = PAGES_PER_BLOCK * PAGE
    any_spec = pl.BlockSpec(memory_space=pl.ANY)
    wspec = lambda a, b: pl.BlockSpec((None, None, a, b), lambda i, pg, be, nu: (layer, be[i], 0, 0))
    grid_spec = pltpu.PrefetchScalarGridSpec(
        num_scalar_prefetch=3,
        grid=(nblk,),
        in_specs=[any_spec, wspec(d, de), wspec(d, de), wspec(de, d)],
        out_specs=any_spec,
        scratch_shapes=[
            pltpu.VMEM((2, rows, d), BF16),
            pltpu.VMEM((2, rows, d), BF16),
            pltpu.VMEM((d, de), BF16),
            pltpu.VMEM((d, de), BF16),
            pltpu.VMEM((de, d), BF16),
            pltpu.SemaphoreType.DMA((2,)),
            pltpu.SemaphoreType.DMA((2,)),
        ],
    )
    return pl.pallas_call(
        functools.partial(_expert_kernel, nblk=nblk, spare_page0=spare_page0),
        grid_spec=grid_spec,
        out_shape=jax.ShapeDtypeStruct((n_rows, d), BF16),
        input_output_aliases={3: 0},
        compiler_params=_cparams(("arbitrary",)),
        name="moe_experts",
    )(slot_page, block_e, n_used, xs, e_gate, e_up, e_down)


def _moe_combine_kernel(ys_ref, rt_ref, h_ref, m_ref, lng_ref, lnb_ref, o_ref, *, alpha):
    tm = h_ref.shape[0]
    rows = ys_ref.shape[0]
    ti = lax.broadcasted_iota(jnp.int32, (tm, tm), 0)
    tj = lax.broadcasted_iota(jnp.int32, (tm, tm), 1)
    eye = jnp.where(ti == tj, 1.0, 0.0).astype(BF16)
    cols = sum(lax.dot_general(eye, piece, (((1,), (1,)), ((), ())), preferred_element_type=F32)
               for piece in _split3(rt_ref[...]))
    lane = lax.broadcasted_iota(jnp.int32, (tm, rows), 1).astype(F32)
    ys = ys_ref[...]
    y = None
    for k in range(TOP_K):
        sel = jnp.where(lane == cols[:, TOP_K + k:TOP_K + k + 1], 1.0, 0.0).astype(BF16)
        part = cols[:, k:k + 1] * jnp.dot(sel, ys, preferred_element_type=F32)
        y = part if y is None else y + part
    o_ref[...] = _residual_ln(h_ref[...], y, m_ref, G2, lng_ref, lnb_ref, alpha)


def _moe_combine(ys, rt, h, mod, lng, lnb, tm, n_lat, batch, alpha, t_out):
    t, d = t_out, h.shape[1]
    rows = _local_rows(tm)
    row = pl.BlockSpec((tm, d), lambda i: (i, 0))
    return pl.pallas_call(
        functools.partial(_moe_combine_kernel, alpha=alpha),
        grid=(t // tm,),
        in_specs=[pl.BlockSpec((rows, d), lambda i: (i, 0)), pl.BlockSpec((SUBLANES, tm), lambda i: (0, i)),
                  row, _mod_spec(d, n_lat // tm, batch),
                  pl.BlockSpec((1, d), lambda i: (0, 0)), pl.BlockSpec((1, d), lambda i: (0, 0))],
        out_specs=row,
        out_shape=jax.ShapeDtypeStruct((t, d), F32),
        compiler_params=_cparams(("parallel",)),
        name="moe_combine_ln",
    )(ys, rt, h, mod, lng, lnb)


def _page_table(npg, pages_per_local):
    nb = npg.shape[0]
    ppb = PAGES_PER_BLOCK
    max_pages = nb * (pages_per_local - 1) + N_EXPERTS * (ppb - 1)
    nblk = -(-max_pages // ppb)
    lstart = jnp.cumsum(npg, 1) - npg
    pe = jnp.sum(npg, 0)
    pe_pad = (pe + ppb - 1) // ppb * ppb
    eend = jnp.cumsum(pe_pad)
    estart = eend - pe_pad
    starts = jnp.arange(nblk, dtype=jnp.int32) * ppb
    block_e = jnp.minimum(jnp.sum((eend[None, :] <= starts[:, None]).astype(jnp.int32), 1), N_EXPERTS - 1)
    n_used = (eend[-1:] // ppb).astype(jnp.int32)
    onehot = block_e[:, None] == jnp.arange(N_EXPERTS, dtype=jnp.int32)[None, :]
    pick = lambda m: jnp.sum(jnp.where(onehot[:, :, None], m.T[None], 0), 1)
    run_pages = pick(npg)
    run_slot = pick(estart[None, :] + jnp.cumsum(npg, 0) - npg)
    run_page = pick(jnp.arange(nb, dtype=jnp.int32)[:, None] * pages_per_local + lstart)
    s = (starts[:, None] + jnp.arange(ppb, dtype=jnp.int32)[None, :])[:, :, None]
    in_run = (run_slot[:, None, :] <= s) & (s < (run_slot + run_pages)[:, None, :])
    page = jnp.sum(jnp.where(in_run, (run_page - run_slot)[:, None, :] + s, 0), 2)
    slot_page = jnp.where(jnp.any(in_run, 2), page, -1).astype(jnp.int32).reshape(-1)
    return slot_page, block_e, n_used


def _moe_layer(h, mod, w_router, router_bias, e_gate, e_up, e_down, layer, lng, lnb, dims, alpha, t_out):
    batch, n_lat, n_ctx, tm = dims
    xs, rt, npg = _router(h, mod, w_router, router_bias, tm, n_lat, batch)
    npg = npg[:, 0].reshape(-1, N_EXPERTS).astype(jnp.int32)
    pages_per_local = _local_rows(tm) // PAGE
    slot_page, block_e, n_used = _page_table(npg, pages_per_local)
    ys = _experts(xs, slot_page, block_e, n_used, e_gate, e_up, e_down, layer, npg.shape[0] * pages_per_local)
    return _moe_combine(ys, rt, h, mod, lng, lnb, tm, n_lat, batch, alpha, t_out)


def kernel(x, c, ctx, c_ctx, w_ada, b_ada, ln_g, ln_b, a_wqkv, a_qg, a_kg, a_wo, b_win, b_vg, b_vb, b_ws, b_bs,
           b_wout, c_win, c_conv, c_alog, c_dtb, c_ng, c_wout, w_router, router_bias, e_gate, e_up, e_down):
    batch, n_lat, d = x.shape
    n_ctx = ctx.shape[1]
    depth = w_ada.shape[0]
    alpha = (2 * depth) ** 0.25
    assert batch + 1 <= SUBLANES
    tm = _pick_tile(512, n_lat, batch * n_ctx)
    dims = (batch, n_lat, n_ctx, tm)

    h = jnp.concatenate([x.reshape(batch * n_lat, d), ctx.reshape(batch * n_ctx, d)], 0).astype(F32)
    cond = jnp.concatenate([c, c_ctx[None], jnp.zeros((SUBLANES - batch - 1, d), c.dtype)], 0).astype(F32)
    ada = _ada_table(cond, w_ada, b_ada)
    mods = jnp.pad(ada.reshape(depth, SUBLANES, 6, d), ((0, 0), (0, 0), (0, SUBLANES - 6), (0, 0)))
    rope = _rope_tables(n_lat, tm)

    for i in range(depth):
        m, j = i % N_MIXERS, i // N_MIXERS
        mod = mods[i]
        lng1, lnb1 = ln_g[i, 0][None], ln_b[i, 0][None]
        lng2, lnb2 = ln_g[i, 1][None], ln_b[i, 1][None]
        if m == 0:
            h = _attention_layer(h, mod, a_wqkv[j], a_qg[j], a_kg[j], a_wo[j], lng1, lnb1, rope, dims, alpha)
        elif m == 1:
            h = _gmlp_layer(h, mod, b_win[j], b_vg[j], b_vb[j], b_ws[j], b_bs[j], b_wout[j], lng1, lnb1, dims, alpha)
        else:
            h = _deltanet_layer(h, mod, c_win[j], c_conv[j], c_alog[j], c_dtb[j], c_ng[j], c_wout[j],
                                lng1, lnb1, dims, alpha)
        t_out = h.shape[0] if i + 1 < depth else batch * n_lat
        h = _moe_layer(h, mod, w_router, router_bias, e_gate, e_up, e_down, i, lng2, lnb2, dims, alpha, t_out)
    return h.reshape(batch, n_lat, d).astype(x.dtype)
```
